```python
import jax, jax.numpy as jnp
from jax import lax
import numpy as np

D_MODEL = 2048
BATCH = 2
SEQ = 16384
DEPTH = 2

HEAD_DIM = 128
CHUNK = 128
A_GROUPS = D_MODEL // 256
D_A = A_GROUPS * HEAD_DIM
ATTN_PATTERNS = ((128, 1), (512, 4), (2048, 16))
HEADS_PER_GROUP = D_MODEL // 512
B_HEADS = HEADS_PER_GROUP * len(ATTN_PATTERNS)
D_B = B_HEADS * HEAD_DIM
D_B_OUT = HEADS_PER_GROUP * HEAD_DIM
D_FF = 4 * D_MODEL
Q_BLOCK = 128
ALIBI_MAX_EXP = 8.0
EPS = 1e-6
SPLITS = (D_A, 2 * D_A, 2 * D_A + D_B, 2 * D_A + 2 * D_B, 2 * D_A + 3 * D_B,
          2 * D_A + 3 * D_B + D_MODEL)
IN_COLS = 2 * D_A + 3 * D_B + 2 * D_MODEL

kernel_name = 'hybrid_gmlp_dilated_attn_gated_block'


def _rmsnorm(x, g):
    xf = x.astype(jnp.float32)
    y = xf * lax.rsqrt(jnp.mean(xf * xf, axis=-1, keepdims=True) + EPS)
    return (y * g.astype(jnp.float32)).astype(x.dtype)


def _layernorm(x, g, b):
    xf = x.astype(jnp.float32)
    mu = jnp.mean(xf, axis=-1, keepdims=True)
    var = jnp.mean(jnp.square(xf - mu), axis=-1, keepdims=True)
    y = (xf - mu) * lax.rsqrt(var + EPS)
    return (y * g.astype(jnp.float32) + b.astype(jnp.float32)).astype(x.dtype)


def _modulate(h, shift, scale):
    return h * (1 + scale[:, None, :]) + shift[:, None, :]


def _spatial_gating(u, v, w_s, b_s):
    bsz, seq, _ = v.shape
    n_chunks = seq // CHUNK
    causal = jnp.tril(jnp.ones((CHUNK, CHUNK), dtype=w_s.dtype))
    vc = v.reshape(bsz, n_chunks, CHUNK, A_GROUPS, HEAD_DIM)
    fv = jnp.einsum('gts,bnsgc->bntgc', w_s * causal, vc) + b_s.T[None, None, :, :, None]
    return u * fv.reshape(bsz, seq, D_A)


def _dilated_window_attention(q, k, v, window, dilation, slopes):
    bsz, seq, nh, hd = q.shape
    span = window // dilation
    m = seq // dilation
    nb = -(-m // Q_BLOCK)
    mp = nb * Q_BLOCK

    def to_blocks(t):
        t = t.reshape(bsz, m, dilation, nh, hd).transpose(0, 2, 3, 1, 4)
        t = jnp.pad(t, ((0, 0), (0, 0), (0, 0), (0, mp - m), (0, 0)))
        return t.reshape(bsz, dilation, nh, nb, Q_BLOCK, hd)

    def with_prev(t):
        prev = jnp.concatenate([jnp.zeros_like(t[:, :, :, :1]), t[:, :, :, :-1]], axis=3)
        return jnp.concatenate([prev, t], axis=4)

    qb = to_blocks(q * HEAD_DIM ** -0.5)
    kb = with_prev(to_blocks(k))
    vb = with_prev(to_blocks(v))
    scores = jnp.einsum('brhnqc,brhnkc->brhnqk', qb, kb, preferred_element_type=jnp.float32)
    qi = jnp.arange(Q_BLOCK)[:, None] + Q_BLOCK
    ki = jnp.arange(2 * Q_BLOCK)[None, :]
    dist = qi - ki
    blk = jnp.arange(nb)[:, None, None]
    valid = (dist >= 0) & (dist <= span) & (blk * Q_BLOCK + ki - Q_BLOCK >= 0)
    alibi = -slopes[:, None, None] * (dilation * dist).astype(jnp.float32)
    scores = jnp.where(valid[None, None, None], scores + alibi[None, None, :, None], -jnp.inf)
    lse = jax.nn.logsumexp(scores, axis=-1)
    probs = jnp.exp(scores - lse[..., None])
    out = jnp.einsum('brhnqk,brhnkc->brhnqc', probs, vb.astype(jnp.float32))
    out = out.reshape(bsz, dilation, nh, mp, hd)[:, :, :, :m]
    out = out.transpose(0, 3, 1, 2, 4).reshape(bsz, seq, nh, hd)
    lse = lse.reshape(bsz, dilation, nh, mp)[:, :, :, :m]
    lse = lse.transpose(0, 3, 1, 2).reshape(bsz, seq, nh)
    return out, lse


def _hybrid_mixer(h, w_in, b_in, g_v, b_v, w_s, b_s, w_oa, w_ob, w_out):
    bsz, seq, _ = h.shape
    proj = h @ w_in + b_in
    u_a, v_a, q, k, v, gate_a, gate_b = jnp.split(proj, SPLITS, axis=-1)
    y_a = _spatial_gating(jax.nn.gelu(u_a), _layernorm(jax.nn.gelu(v_a), g_v, b_v), w_s, b_s)
    q = q.reshape(bsz, seq, B_HEADS, HEAD_DIM)
    k = k.reshape(bsz, seq, B_HEADS, HEAD_DIM)
    v = v.reshape(bsz, seq, B_HEADS, HEAD_DIM)
    slopes = 2.0 ** (-ALIBI_MAX_EXP * jnp.arange(1, B_HEADS + 1, dtype=jnp.float32) / B_HEADS)
    outs, lses = [], []
    for g, (window, dilation) in enumerate(ATTN_PATTERNS):
        hs = slice(g * HEADS_PER_GROUP, (g + 1) * HEADS_PER_GROUP)
        o, l = _dilated_window_attention(q[:, :, hs], k[:, :, hs], v[:, :, hs],
                                         window, dilation, slopes[hs])
        outs.append(o)
        lses.append(l)
    weights = jax.nn.softmax(jnp.stack(lses), axis=0)
    y_b = jnp.sum(weights[..., None] * jnp.stack(outs), axis=0)
    y_b = y_b.reshape(bsz, seq, D_B_OUT).astype(h.dtype)
    merged = jax.nn.sigmoid(gate_a) * (y_a @ w_oa) + jax.nn.sigmoid(gate_b) * (y_b @ w_ob)
    return merged @ w_out


def _sq_relu_mlp(h, w1, b1, w2, b2):
    return jnp.square(jax.nn.relu(h @ w1 + b1)) @ w2 + b2


def setup_inputs(seed: int = 0) -> dict:
    key = jax.random.key(seed)
    ks = jax.random.split(key, 24)
    L = DEPTH
    f32 = jnp.float32

    def nrm(k, shape, fan_in):
        return jax.random.normal(k, shape, f32) * fan_in ** -0.5

    def small(k, shape):
        return 0.02 * jax.random.normal(k, shape, f32)

    return {
        'x': jax.random.normal(ks[0], (BATCH, SEQ, D_MODEL), f32),
        'c': jax.random.normal(ks[1], (BATCH, D_MODEL), f32),
        'w_ada': 0.5 * nrm(ks[2], (L, D_MODEL, 6 * D_MODEL), D_MODEL),
        'b_ada': small(ks[3], (L, 6 * D_MODEL)),
        'g_mix': 1.0 + small(ks[4], (L, D_MODEL)),
        'w_in': nrm(ks[5], (L, D_MODEL, IN_COLS), D_MODEL),
        'b_in': small(ks[6], (L, IN_COLS)),
        'g_v': 1.0 + small(ks[7], (L, D_A)),
        'b_v': small(ks[8], (L, D_A)),
        'w_s': nrm(ks[9], (L, A_GROUPS, CHUNK, CHUNK), CHUNK),
        'b_s': 1.0 + small(ks[10], (L, A_GROUPS, CHUNK)),
        'w_oa': nrm(ks[11], (L, D_A, D_MODEL), D_A),
        'w_ob': nrm(ks[12], (L, D_B_OUT, D_MODEL), D_B_OUT),
        'w_out': nrm(ks[13], (L, D_MODEL, D_MODEL), D_MODEL),
        'g_mlp': 1.0 + small(ks[14], (L, D_MODEL)),
        'w1': nrm(ks[15], (L, D_MODEL, D_FF), D_MODEL),
        'b1': small(ks[16], (L, D_FF)),
        'w2': nrm(ks[17], (L, D_FF, D_MODEL), D_FF),
        'b2': small(ks[18], (L, D_MODEL)),
        'g_final': 1.0 + small(ks[19], (D_MODEL,)),
    }


def reference(x, c, w_ada, b_ada, g_mix, w_in, b_in, g_v, b_v, w_s, b_s, w_oa, w_ob,
              w_out, g_mlp, w1, b1, w2, b2, g_final):
    c_act = jax.nn.silu(c)
    for l in range(DEPTH):
        mod = c_act @ w_ada[l] + b_ada[l]
        sh1, sc1, gt1, sh2, sc2, gt2 = jnp.split(mod, 6, axis=-1)
        h = _modulate(_rmsnorm(x, g_mix[l]), sh1, sc1)
        x = x + gt1[:, None, :] * _hybrid_mixer(h, w_in[l], b_in[l], g_v[l], b_v[l], w_s[l],
                                                b_s[l], w_oa[l], w_ob[l], w_out[l])
        h = _modulate(_rmsnorm(x, g_mlp[l]), sh2, sc2)
        x = x + gt2[:, None, :] * _sq_relu_mlp(h, w1[l], b1[l], w2[l], b2[l])
    return _rmsnorm(x, g_final)
```

```python
import functools

import jax
import jax.numpy as jnp
from jax import lax
from jax.experimental import pallas as pl
from jax.experimental.pallas import tpu as pltpu

F32 = jnp.float32
BF16 = jnp.bfloat16

HEAD_DIM = 128
CHUNK = 128
Q_BLOCK = 128
SPAN = 128
DILATIONS = (1, 4, 16)
HEADS_PER_GROUP = 4
ALIBI_MAX_EXP = 8.0
EPS = 1e-6
ATTN_TILE = Q_BLOCK * max(DILATIONS)
NEG_INF = float("-inf")
VMEM_LIMIT = 56 * 1024 * 1024


def _params(*sem):
    return pltpu.CompilerParams(dimension_semantics=sem, vmem_limit_bytes=VMEM_LIMIT)


def _ada_body(c_ref, w_ref, b_ref, o_ref):
    c = c_ref[...]
    act = (c * jax.nn.sigmoid(c)).astype(BF16)
    o_ref[...] = jnp.dot(act, w_ref[...].astype(BF16), preferred_element_type=F32) + b_ref[...]


def _ada(c, w_ada, b_ada):
    depth, d, n = w_ada.shape
    bsz = c.shape[0]
    tn = 1024
    return pl.pallas_call(
        _ada_body,
        grid=(depth, n // tn),
        in_specs=[
            pl.BlockSpec((bsz, d), lambda l, j: (0, 0)),
            pl.BlockSpec((None, d, tn), lambda l, j: (l, 0, j)),
            pl.BlockSpec((None, 1, tn), lambda l, j: (l, 0, j)),
        ],
        out_specs=pl.BlockSpec((None, bsz, tn), lambda l, j: (l, 0, j)),
        out_shape=jax.ShapeDtypeStruct((depth, bsz, n), F32),
        compiler_params=_params("arbitrary", "arbitrary"),
        name="ada",
    )(c, w_ada, b_ada.reshape(depth, 1, n))


def _norm_modulate(h_ref, x_ref, g_ref, sh_ref, sc_ref, rows=32):
    g = g_ref[...]
    sc1 = 1.0 + sc_ref[...]
    sh = sh_ref[...]

    def body(i, carry):
        r0 = pl.multiple_of(i * rows, rows)
        xv = x_ref[pl.ds(r0, rows), :]
        ms = jnp.mean(xv * xv, axis=-1, keepdims=True)
        y = xv * lax.rsqrt(ms + EPS) * g
        h_ref[pl.ds(r0, rows), :] = (y * sc1 + sh).astype(h_ref.dtype)
        return carry

    lax.fori_loop(0, x_ref.shape[0] // rows, body, 0)


def _mod_spec(d, layer, bsz, comp, tiles_per_batch):
    return pl.BlockSpec(
        (None, 1, d),
        lambda i, *_: ((layer * bsz + i // tiles_per_batch) * 6 + comp, 0, 0))


def _inproj_body(x_ref, g_ref, sh_ref, sc_ref, w_ref, b_ref, o_ref, h_ref):
    @pl.when(pl.program_id(1) == 0)
    def _():
        _norm_modulate(h_ref, x_ref, g_ref, sh_ref, sc_ref)

    acc = jnp.dot(h_ref[...], w_ref[...], preferred_element_type=F32)
    o_ref[...] = (acc + b_ref[...]).astype(o_ref.dtype)


def _in_proj(xf, g, modr, w, b, layer, bsz, tm=1024, tn=1536):
    m, d = xf.shape
    n = w.shape[1]
    tpb = (m // bsz) // tm
    return pl.pallas_call(
        _inproj_body,
        grid=(m // tm, n // tn),
        in_specs=[
            pl.BlockSpec((tm, d), lambda i, j: (i, 0)),
            pl.BlockSpec((1, d), lambda i, j: (0, 0)),
            _mod_spec(d, layer, bsz, 0, tpb),
            _mod_spec(d, layer, bsz, 1, tpb),
            pl.BlockSpec((d, tn), lambda i, j: (0, j)),
            pl.BlockSpec((1, tn), lambda i, j: (0, j)),
        ],
        out_specs=pl.BlockSpec((tm, tn), lambda i, j: (i, j)),
        out_shape=jax.ShapeDtypeStruct((m, n), BF16),
        scratch_shapes=[pltpu.VMEM((tm, d), BF16)],
        compiler_params=_params("arbitrary", "arbitrary"),
        name="in_proj",
    )(xf, g.reshape(1, d), modr, modr, w, b.reshape(1, n))


def _gmlp_body(u_ref, v_ref, gv_ref, bv_ref, ws_ref, bst_ref, o_ref):
    groups = ws_ref.shape[0]
    n_chunks = u_ref.shape[0] // CHUNK
    rowi = lax.broadcasted_iota(jnp.int32, (CHUNK, CHUNK), 0)
    coli = lax.broadcasted_iota(jnp.int32, (CHUNK, CHUNK), 1)
    causal = rowi >= coli
    gv = gv_ref[...]
    bv = bv_ref[...]
    for c in range(n_chunks):
        rs = slice(c * CHUNK, (c + 1) * CHUNK)
        v = jax.nn.gelu(v_ref[rs, :].astype(F32))
        mu = jnp.mean(v, axis=-1, keepdims=True)
        vc = v - mu
        var = jnp.mean(vc * vc, axis=-1, keepdims=True)
        vn = (vc * lax.rsqrt(var + EPS) * gv + bv).astype(BF16)
        for g in range(groups):
            cs = slice(g * HEAD_DIM, (g + 1) * HEAD_DIM)
            w = jnp.where(causal, ws_ref[g], 0.0).astype(BF16)
            fv = jnp.dot(w, vn[:, cs], preferred_element_type=F32) + bst_ref[:, g:g + 1]
            u = jax.nn.gelu(u_ref[rs, cs].astype(F32))
            o_ref[rs, cs] = (u * fv).astype(o_ref.dtype)


def _gmlp(proj, g_v, b_v, w_s, b_s, tm=512):
    m = proj.shape[0]
    groups = w_s.shape[0]
    d_a = groups * HEAD_DIM
    return pl.pallas_call(
        _gmlp_body,
        grid=(m // tm,),
        in_specs=[
            pl.BlockSpec((tm, d_a), lambda i: (i, 0)),
            pl.BlockSpec((tm, d_a), lambda i: (i, 1)),
            pl.BlockSpec((1, d_a), lambda i: (0, 0)),
            pl.BlockSpec((1, d_a), lambda i: (0, 0)),
            pl.BlockSpec((groups, CHUNK, CHUNK), lambda i: (0, 0, 0)),
            pl.BlockSpec((CHUNK, groups), lambda i: (0, 0)),
        ],
        out_specs=pl.BlockSpec((tm, d_a), lambda i: (i, 0)),
        out_shape=jax.ShapeDtypeStruct((m, d_a), BF16),
        compiler_params=_params("arbitrary"),
        name="gmlp",
    )(proj, proj, g_v.reshape(1, d_a), b_v.reshape(1, d_a), w_s, b_s.T)


def _attn_body(slopes_ref, *refs):
    ng = len(DILATIONS)
    q_refs = refs[0:ng]
    kc_refs = refs[ng:2 * ng]
    kp_refs = refs[2 * ng:3 * ng]
    vc_refs = refs[3 * ng:4 * ng]
    vp_refs = refs[4 * ng:5 * ng]
    o_ref = refs[5 * ng]
    qbuf, kbuf, vbuf = refs[5 * ng + 1:5 * ng + 4]
    u_scr = refs[5 * ng + 4:5 * ng + 4 + ng]
    m_scr = refs[5 * ng + 4 + ng:5 * ng + 4 + 2 * ng]
    l_scr = refs[5 * ng + 4 + 2 * ng:5 * ng + 4 + 3 * ng]

    tile = pl.program_id(1)
    hg = pl.program_id(2)
    t = ATTN_TILE
    scale = HEAD_DIM ** -0.5

    qi = lax.broadcasted_iota(jnp.int32, (Q_BLOCK, 2 * Q_BLOCK), 0) + Q_BLOCK
    ki = lax.broadcasted_iota(jnp.int32, (Q_BLOCK, 2 * Q_BLOCK), 1)
    dist = qi - ki
    in_band = (dist >= 0) & (dist <= SPAN)
    before_start = ki < Q_BLOCK

    for g, d in enumerate(DILATIONS):
        nblk = t // (Q_BLOCK * d)
        back = Q_BLOCK * d
        slope = slopes_ref[g * HEADS_PER_GROUP + hg]
        bias = jnp.where(in_band, -slope * (d * dist).astype(F32), NEG_INF)

        qbuf[...] = q_refs[g][...].astype(F32)
        kbuf[0:back, :] = kp_refs[g][t - back:t, :].astype(F32)
        kbuf[back:back + t, :] = kc_refs[g][...].astype(F32)
        vbuf[0:back, :] = vp_refs[g][t - back:t, :].astype(F32)
        vbuf[back:back + t, :] = vc_refs[g][...].astype(F32)

        def block(idx, carry, d=d, nblk=nblk, bias=bias, g=g):
            r = idx // nblk
            n = idx % nblk
            start = r + (d * Q_BLOCK) * n
            if d == 1:
                start = pl.multiple_of(start, Q_BLOCK)
                q_rows = pl.ds(start, Q_BLOCK)
                kv_rows = pl.ds(start, 2 * Q_BLOCK)
            else:
                q_rows = pl.ds(start, Q_BLOCK, stride=d)
                kv_rows = pl.ds(start, 2 * Q_BLOCK, stride=d)
            qb = qbuf[q_rows, :].astype(BF16)
            kb = kbuf[kv_rows, :].astype(BF16)
            vb = vbuf[kv_rows, :].astype(BF16)
            s = lax.dot_general(qb, kb, (((1,), (1,)), ((), ())),
                                preferred_element_type=F32)
            s = s * scale + bias
            at_start = jnp.logical_and(tile == 0, n == 0)
            s = jnp.where(jnp.logical_and(at_start, before_start), NEG_INF, s)
            mx = jnp.max(s, axis=-1, keepdims=True)
            p = jnp.exp(s - mx)
            den = jnp.sum(p, axis=-1, keepdims=True)
            u = jnp.dot(p.astype(BF16), vb, preferred_element_type=F32)
            u_scr[g][q_rows, :] = u
            m_scr[g][q_rows, :] = jnp.broadcast_to(mx, (Q_BLOCK, HEAD_DIM))
            l_scr[g][q_rows, :] = jnp.broadcast_to(den, (Q_BLOCK, HEAD_DIM))
            return carry

        lax.fori_loop(0, t // Q_BLOCK, block, 0)

    def combine(i, carry):
        rows = pl.ds(pl.multiple_of(i * Q_BLOCK, Q_BLOCK), Q_BLOCK)
        ms = [m_scr[g][rows, :] for g in range(ng)]
        top = functools.reduce(jnp.maximum, ms)
        num = jnp.zeros((Q_BLOCK, HEAD_DIM), F32)
        den = jnp.zeros((Q_BLOCK, HEAD_DIM), F32)
        for g in range(ng):
            w = jnp.exp(ms[g] - top)
            num = num + w * u_scr[g][rows, :]
            den = den + w * l_scr[g][rows, :]
        o_ref[rows, :] = (num / den).astype(o_ref.dtype)
        return carry

    lax.fori_loop(0, t // Q_BLOCK, combine, 0)


def _attention(proj, slopes, bsz, col0, d_b):
    m = proj.shape[0]
    t = ATTN_TILE
    tiles = (m // bsz) // t
    ng = len(DILATIONS)
    hb = col0 // HEAD_DIM
    heads = d_b // HEAD_DIM

    def cur(base, g):
        return pl.BlockSpec(
            (t, HEAD_DIM),
            lambda b, i, h, s: (b * tiles + i, base + g * HEADS_PER_GROUP + h))

    def prev(base, g):
        return pl.BlockSpec(
            (t, HEAD_DIM),
            lambda b, i, h, s: (b * tiles + jnp.maximum(i - 1, 0),
                                base + g * HEADS_PER_GROUP + h))

    in_specs = ([cur(hb, g) for g in range(ng)]
                + [cur(hb + heads, g) for g in range(ng)]
                + [prev(hb + heads, g) for g in range(ng)]
                + [cur(hb + 2 * heads, g) for g in range(ng)]
                + [prev(hb + 2 * heads, g) for g in range(ng)])
    kv_rows = t + Q_BLOCK * max(DILATIONS)
    scratch = ([pltpu.VMEM((t, HEAD_DIM), F32),
                pltpu.VMEM((kv_rows, HEAD_DIM), F32),
                pltpu.VMEM((kv_rows, HEAD_DIM), F32)]
               + [pltpu.VMEM((t, HEAD_DIM), F32) for _ in range(3 * ng)])
    grid_spec = pltpu.PrefetchScalarGridSpec(
        num_scalar_prefetch=1,
        grid=(bsz, tiles, HEADS_PER_GROUP),
        in_specs=in_specs,
        out_specs=pl.BlockSpec((t, HEAD_DIM), lambda b, i, h, s: (b * tiles + i, h)),
        scratch_shapes=scratch,
    )
    return pl.pallas_call(
        _attn_body,
        grid_spec=grid_spec,
        out_shape=jax.ShapeDtypeStruct((m, HEADS_PER_GROUP * HEAD_DIM), BF16),
        compiler_params=_params("arbitrary", "arbitrary", "arbitrary"),
        name="attention",
    )(slopes, *([proj] * (5 * ng)))


def _outproj_body(ya_ref, yb_ref, ga_ref, gb_ref, x_ref, gt_ref, woa_ref, wob_ref, wout_ref,
                  o_ref):
    a = jnp.dot(ya_ref[...], woa_ref[...], preferred_element_type=F32)
    b = jnp.dot(yb_ref[...], wob_ref[...], preferred_element_type=F32)
    merged = (jax.nn.sigmoid(ga_ref[...].astype(F32)) * a
              + jax.nn.sigmoid(gb_ref[...].astype(F32)) * b)
    o = jnp.dot(merged.astype(BF16), wout_ref[...], preferred_element_type=F32)
    o_ref[...] = x_ref[...] + gt_ref[...] * o


def _out_proj(ya, yb, proj, xf, modr, w_oa, w_ob, w_out, layer, bsz, gate_col0, tm=512):
    m, d = xf.shape
    d_a = ya.shape[1]
    d_bo = yb.shape[1]
    tpb = (m // bsz) // tm
    gblk = gate_col0 // d
    const = lambda i: (0, 0)
    return pl.pallas_call(
        _outproj_body,
        grid=(m // tm,),
        in_specs=[
            pl.BlockSpec((tm, d_a), lambda i: (i, 0)),
            pl.BlockSpec((tm, d_bo), lambda i: (i, 0)),
            pl.BlockSpec((tm, d), lambda i: (i, gblk)),
            pl.BlockSpec((tm, d), lambda i: (i, gblk + 1)),
            pl.BlockSpec((tm, d), lambda i: (i, 0)),
            _mod_spec(d, layer, bsz, 2, tpb),
            pl.BlockSpec((d_a, d), const, pipeline_mode=pl.Buffered(1)),
            pl.BlockSpec((d_bo, d), const, pipeline_mode=pl.Buffered(1)),
            pl.BlockSpec((d, d), const, pipeline_mode=pl.Buffered(1)),
        ],
        out_specs=pl.BlockSpec((tm, d), lambda i: (i, 0)),
        out_shape=jax.ShapeDtypeStruct((m, d), F32),
        compiler_params=_params("arbitrary"),
        name="out_proj",
    )(ya, yb, proj, proj, xf, modr, w_oa, w_ob, w_out)


def _mlp_body(x_ref, g_ref, sh_ref, sc_ref, gt_ref, w1_ref, b1_ref, w2_ref, b2_ref, gf_ref,
              o_ref, h_ref, acc_ref, *, final_norm, rows=32):
    f = pl.program_id(1)

    @pl.when(f == 0)
    def _():
        _norm_modulate(h_ref, x_ref, g_ref, sh_ref, sc_ref)

    hid = jnp.dot(h_ref[...], w1_ref[...], preferred_element_type=F32) + b1_ref[...]
    hid = jnp.square(jnp.maximum(hid, 0.0)).astype(BF16)
    part = jnp.dot(hid, w2_ref[...], preferred_element_type=F32)

    @pl.when(f == 0)
    def _():
        acc_ref[...] = part

    @pl.when(f > 0)
    def _():
        acc_ref[...] += part

    @pl.when(f == pl.num_programs(1) - 1)
    def _():
        gt = gt_ref[...]
        b2 = b2_ref[...]
        gf = gf_ref[...]

        def body(i, carry):
            r = pl.ds(pl.multiple_of(i * rows, rows), rows)
            y = x_ref[r, :] + gt * (acc_ref[r, :] + b2)
            if final_norm:
                ms = jnp.mean(y * y, axis=-1, keepdims=True)
                y = y * lax.rsqrt(ms + EPS) * gf
            o_ref[r, :] = y
            return carry

        lax.fori_loop(0, x_ref.shape[0] // rows, body, 0)


def _mlp(xf, g, modr, w1, b1, w2, b2, g_final, layer, bsz, final_norm, tm=512, tf=512):
    m, d = xf.shape
    dff = w1.shape[1]
    tpb = (m // bsz) // tm
    return pl.pallas_call(
        functools.partial(_mlp_body, final_norm=final_norm),
        grid=(m // tm, dff // tf),
        in_specs=[
            pl.BlockSpec((tm, d), lambda i, f: (i, 0)),
            pl.BlockSpec((1, d), lambda i, f: (0, 0)),
            _mod_spec(d, layer, bsz, 3, tpb),
            _mod_spec(d, layer, bsz, 4, tpb),
            _mod_spec(d, layer, bsz, 5, tpb),
            pl.BlockSpec((d, tf), lambda i, f: (0, f)),
            pl.BlockSpec((1, tf), lambda i, f: (0, f)),
            pl.BlockSpec((tf, d), lambda i, f: (f, 0)),
            pl.BlockSpec((1, d), lambda i, f: (0, 0)),
            pl.BlockSpec((1, d), lambda i, f: (0, 0)),
        ],
        out_specs=pl.BlockSpec((tm, d), lambda i, f: (i, 0)),
        out_shape=jax.ShapeDtypeStruct((m, d), F32),
        scratch_shapes=[pltpu.VMEM((tm, d), BF16), pltpu.VMEM((tm, d), F32)],
        compiler_params=_params("arbitrary", "arbitrary"),
        name="mlp",
    )(xf, g.reshape(1, d), modr, modr, modr, w1, b1.reshape(1, dff), w2, b2.reshape(1, d),
      g_final.reshape(1, d))


def kernel(x, c, w_ada, b_ada, g_mix, w_in, b_in, g_v, b_v, w_s, b_s, w_oa, w_ob, w_out,
           g_mlp, w1, b1, w2, b2, g_final):
    bsz, seq, d = x.shape
    depth = w_ada.shape[0]
    m = bsz * seq
    d_a = w_oa.shape[1]
    d_bo = w_ob.shape[1]
    d_b = d_bo * len(DILATIONS)
    assert seq % ATTN_TILE == 0 and d_bo == HEADS_PER_GROUP * HEAD_DIM

    qkv0, qkv1 = 2 * d_a, 2 * d_a + 3 * d_b

    def reorder(a):
        return jnp.concatenate([a[..., :qkv0], a[..., qkv1:], a[..., qkv0:qkv1]], axis=-1)

    gate_col0 = 2 * d_a
    qkv_col0 = 2 * d_a + 2 * d

    n_heads = d_b // HEAD_DIM
    slopes = 2.0 ** (-ALIBI_MAX_EXP * jnp.arange(1, n_heads + 1, dtype=F32) / n_heads)

    mod = _ada(c, w_ada, b_ada)
    modr = mod.reshape(depth * bsz * 6, 1, d)
    xf = x.reshape(m, d)
    for l in range(depth):
        w_in_l = reorder(w_in[l]).astype(BF16)
        b_in_l = reorder(b_in[l])
        proj = _in_proj(xf, g_mix[l], modr, w_in_l, b_in_l, l, bsz)
        ya = _gmlp(proj, g_v[l], b_v[l], w_s[l], b_s[l])
        yb = _attention(proj, slopes, bsz, qkv_col0, d_b)
        x1 = _out_proj(ya, yb, proj, xf, modr, w_oa[l].astype(BF16), w_ob[l].astype(BF16),
                       w_out[l].astype(BF16), l, bsz, gate_col0)
        xf = _mlp(x1, g_mlp[l], modr, w1[l].astype(BF16), b1[l], w2[l].astype(BF16), b2[l],
                  g_final, l, bsz, final_norm=(l == depth - 1))
    return xf.reshape(bsz, seq, d)
```

```python
import functools

import jax
import jax.numpy as jnp
from jax import lax
from jax.experimental import pallas as pl
from jax.experimental.pallas import tpu as pltpu

F32 = jnp.float32
BF16 = jnp.bfloat16

HEAD_DIM = 128
CHUNK = 128
Q_BLOCK = 128
SPAN = 128
DILATIONS = (1, 4, 16)
HEADS_PER_GROUP = 4
ALIBI_MAX_EXP = 8.0
EPS = 1e-6
ATTN_TILE = Q_BLOCK * max(DILATIONS)
NEG_INF = float("-inf")
VMEM_LIMIT = 56 * 1024 * 1024


def _params(*sem):
    return pltpu.CompilerParams(dimension_semantics=sem, vmem_limit_bytes=VMEM_LIMIT)


def _ada_body(c_ref, w_ref, b_ref, o_ref):
    c = c_ref[...]
    act = (c * jax.nn.sigmoid(c)).astype(BF16)
    o_ref[...] = jnp.dot(act, w_ref[...].astype(BF16), preferred_element_type=F32) + b_ref[...]


def _ada(c, w_ada, b_ada):
    depth, d, n = w_ada.shape
    bsz = c.shape[0]
    tn = 1024
    return pl.pallas_call(
        _ada_body,
        grid=(depth, n // tn),
        in_specs=[
            pl.BlockSpec((bsz, d), lambda l, j: (0, 0)),
            pl.BlockSpec((None, d, tn), lambda l, j: (l, 0, j)),
            pl.BlockSpec((None, 1, tn), lambda l, j: (l, 0, j)),
        ],
        out_specs=pl.BlockSpec((None, bsz, tn), lambda l, j: (l, 0, j)),
        out_shape=jax.ShapeDtypeStruct((depth, bsz, n), F32),
        compiler_params=_params("arbitrary", "arbitrary"),
        name="ada",
    )(c, w_ada, b_ada.reshape(depth, 1, n))


def _norm_modulate(h_ref, x_ref, g_ref, sh_ref, sc_ref, rows=16, unroll=4):
    gs = g_ref[...] * (1.0 + sc_ref[...])
    sh = sh_ref[...]

    def body(i, carry):
        r0 = pl.multiple_of(i * rows, rows)
        xv = x_ref[pl.ds(r0, rows), :]
        ms = jnp.mean(xv * xv, axis=-1, keepdims=True)
        h_ref[pl.ds(r0, rows), :] = (xv * lax.rsqrt(ms + EPS) * gs + sh).astype(h_ref.dtype)
        return carry

    lax.fori_loop(0, x_ref.shape[0] // rows, body, 0, unroll=unroll)


def _mod_spec(d, layer, bsz, comp, tiles_per_batch):
    return pl.BlockSpec(
        (None, 1, d),
        lambda i, *_: ((layer * bsz + i // tiles_per_batch) * 6 + comp, 0, 0))


def _inproj_body(x_ref, g_ref, sh_ref, sc_ref, w_ref, b_ref, o_ref, h_ref):
    @pl.when(pl.program_id(1) == 0)
    def _():
        _norm_modulate(h_ref, x_ref, g_ref, sh_ref, sc_ref)

    acc = jnp.dot(h_ref[...], w_ref[...], preferred_element_type=F32)
    o_ref[...] = (acc + b_ref[...]).astype(o_ref.dtype)


def _in_proj(xf, g, modr, w, b, layer, bsz, tm=1024, tn=1536):
    m, d = xf.shape
    n = w.shape[1]
    tpb = (m // bsz) // tm
    return pl.pallas_call(
        _inproj_body,
        grid=(m // tm, n // tn),
        in_specs=[
            pl.BlockSpec((tm, d), lambda i, j: (i, 0)),
            pl.BlockSpec((1, d), lambda i, j: (0, 0)),
            _mod_spec(d, layer, bsz, 0, tpb),
            _mod_spec(d, layer, bsz, 1, tpb),
            pl.BlockSpec((d, tn), lambda i, j: (0, j)),
            pl.BlockSpec((1, tn), lambda i, j: (0, j)),
        ],
        out_specs=pl.BlockSpec((tm, tn), lambda i, j: (i, j)),
        out_shape=jax.ShapeDtypeStruct((m, n), BF16),
        scratch_shapes=[pltpu.VMEM((tm, d), BF16)],
        compiler_params=_params("arbitrary", "arbitrary"),
        name="in_proj",
    )(xf, g.reshape(1, d), modr, modr, w, b.reshape(1, n))


def _gmlp_body(u_ref, v_ref, gv_ref, bv_ref, ws_ref, bst_ref, o_ref):
    groups = ws_ref.shape[0]
    n_chunks = u_ref.shape[0] // CHUNK
    rowi = lax.broadcasted_iota(jnp.int32, (CHUNK, CHUNK), 0)
    coli = lax.broadcasted_iota(jnp.int32, (CHUNK, CHUNK), 1)
    causal = rowi >= coli
    gv = gv_ref[...]
    bv = bv_ref[...]
    for c in range(n_chunks):
        rs = slice(c * CHUNK, (c + 1) * CHUNK)
        v = jax.nn.gelu(v_ref[rs, :].astype(F32))
        mu = jnp.mean(v, axis=-1, keepdims=True)
        vc = v - mu
        var = jnp.mean(vc * vc, axis=-1, keepdims=True)
        vn = (vc * lax.rsqrt(var + EPS) * gv + bv).astype(BF16)
        for g in range(groups):
            cs = slice(g * HEAD_DIM, (g + 1) * HEAD_DIM)
            w = jnp.where(causal, ws_ref[g], 0.0).astype(BF16)
            fv = jnp.dot(w, vn[:, cs], preferred_element_type=F32) + bst_ref[:, g:g + 1]
            u = jax.nn.gelu(u_ref[rs, cs].astype(F32))
            o_ref[rs, cs] = (u * fv).astype(o_ref.dtype)


def _gmlp(proj, g_v, b_v, w_s, b_s, tm=512):
    m = proj.shape[0]
    groups = w_s.shape[0]
    d_a = groups * HEAD_DIM
    return pl.pallas_call(
        _gmlp_body,
        grid=(m // tm,),
        in_specs=[
            pl.BlockSpec((tm, d_a), lambda i: (i, 0)),
            pl.BlockSpec((tm, d_a), lambda i: (i, 1)),
            pl.BlockSpec((1, d_a), lambda i: (0, 0)),
            pl.BlockSpec((1, d_a), lambda i: (0, 0)),
            pl.BlockSpec((groups, CHUNK, CHUNK), lambda i: (0, 0, 0)),
            pl.BlockSpec((CHUNK, groups), lambda i: (0, 0)),
        ],
        out_specs=pl.BlockSpec((tm, d_a), lambda i: (i, 0)),
        out_shape=jax.ShapeDtypeStruct((m, d_a), BF16),
        compiler_params=_params("arbitrary"),
        name="gmlp",
    )(proj, proj, g_v.reshape(1, d_a), b_v.reshape(1, d_a), w_s, b_s.T)


def _attn_body(slopes_ref, *refs):
    ng = len(DILATIONS)
    q_refs = refs[0:ng]
    kc_refs = refs[ng:2 * ng]
    kp_refs = refs[2 * ng:3 * ng]
    vc_refs = refs[3 * ng:4 * ng]
    vp_refs = refs[4 * ng:5 * ng]
    o_ref = refs[5 * ng]
    qbuf, kbuf, vbuf = refs[5 * ng + 1:5 * ng + 4]
    u_scr = refs[5 * ng + 4:5 * ng + 4 + ng]
    m_scr = refs[5 * ng + 4 + ng:5 * ng + 4 + 2 * ng]
    l_scr = refs[5 * ng + 4 + 2 * ng:5 * ng + 4 + 3 * ng]

    tile = pl.program_id(1)
    hg = pl.program_id(2)
    t = ATTN_TILE
    scale = HEAD_DIM ** -0.5

    qi = lax.broadcasted_iota(jnp.int32, (Q_BLOCK, 2 * Q_BLOCK), 0) + Q_BLOCK
    ki = lax.broadcasted_iota(jnp.int32, (Q_BLOCK, 2 * Q_BLOCK), 1)
    dist = qi - ki
    in_band = (dist >= 0) & (dist <= SPAN)
    before_start = ki < Q_BLOCK

    for g, d in enumerate(DILATIONS):
        nblk = t // (Q_BLOCK * d)
        back = Q_BLOCK * d
        slope = slopes_ref[g * HEADS_PER_GROUP + hg]
        bias = jnp.where(in_band, -slope * (d * dist).astype(F32), NEG_INF)

        qbuf[...] = q_refs[g][...].astype(F32)
        kbuf[0:back, :] = kp_refs[g][t - back:t, :].astype(F32)
        kbuf[back:back + t, :] = kc_refs[g][...].astype(F32)
        vbuf[0:back, :] = vp_refs[g][t - back:t, :].astype(F32)
        vbuf[back:back + t, :] = vc_refs[g][...].astype(F32)

        def block(idx, carry, d=d, nblk=nblk, bias=bias, g=g):
            r = idx // nblk
            n = idx % nblk
            start = r + (d * Q_BLOCK) * n
            if d == 1:
                start = pl.multiple_of(start, Q_BLOCK)
                q_rows = pl.ds(start, Q_BLOCK)
                kv_rows = pl.ds(start, 2 * Q_BLOCK)
            else:
                q_rows = pl.ds(start, Q_BLOCK, stride=d)
                kv_rows = pl.ds(start, 2 * Q_BLOCK, stride=d)
            qb = qbuf[q_rows, :].astype(BF16)
            kb = kbuf[kv_rows, :].astype(BF16)
            vb = vbuf[kv_rows, :].astype(BF16)
            s = lax.dot_general(qb, kb, (((1,), (1,)), ((), ())),
                                preferred_element_type=F32)
            s = s * scale + bias
            at_start = jnp.logical_and(tile == 0, n == 0)
            s = jnp.where(jnp.logical_and(at_start, before_start), NEG_INF, s)
            mx = jnp.max(s, axis=-1, keepdims=True)
            p = jnp.exp(s - mx)
            den = jnp.sum(p, axis=-1, keepdims=True)
            u = jnp.dot(p.astype(BF16), vb, preferred_element_type=F32)
            u_scr[g][q_rows, :] = u
            m_scr[g][q_rows, :] = jnp.broadcast_to(mx, (Q_BLOCK, HEAD_DIM))
            l_scr[g][q_rows, :] = jnp.broadcast_to(den, (Q_BLOCK, HEAD_DIM))
            return carry

        lax.fori_loop(0, t // Q_BLOCK, block, 0, unroll=True)

    def combine(i, carry):
        rows = pl.ds(pl.multiple_of(i * Q_BLOCK, Q_BLOCK), Q_BLOCK)
        ms = [m_scr[g][rows, :] for g in range(ng)]
        top = functools.reduce(jnp.maximum, ms)
        num = jnp.zeros((Q_BLOCK, HEAD_DIM), F32)
        den = jnp.zeros((Q_BLOCK, HEAD_DIM), F32)
        for g in range(ng):
            w = jnp.exp(ms[g] - top)
            num = num + w * u_scr[g][rows, :]
            den = den + w * l_scr[g][rows, :]
        o_ref[rows, :] = (num / den).astype(o_ref.dtype)
        return carry

    lax.fori_loop(0, t // Q_BLOCK, combine, 0)


def _attention(proj, slopes, bsz, col0, d_b):
    m = proj.shape[0]
    t = ATTN_TILE
    tiles = (m // bsz) // t
    ng = len(DILATIONS)
    hb = col0 // HEAD_DIM
    heads = d_b // HEAD_DIM

    def cur(base, g):
        return pl.BlockSpec(
            (t, HEAD_DIM),
            lambda b, i, h, s: (b * tiles + i, base + g * HEADS_PER_GROUP + h))

    def prev(base, g):
        return pl.BlockSpec(
            (t, HEAD_DIM),
            lambda b, i, h, s: (b * tiles + jnp.maximum(i - 1, 0),
                                base + g * HEADS_PER_GROUP + h))

    in_specs = ([cur(hb, g) for g in range(ng)]
                + [cur(hb + heads, g) for g in range(ng)]
                + [prev(hb + heads, g) for g in range(ng)]
                + [cur(hb + 2 * heads, g) for g in range(ng)]
                + [prev(hb + 2 * heads, g) for g in range(ng)])
    kv_rows = t + Q_BLOCK * max(DILATIONS)
    scratch = ([pltpu.VMEM((t, HEAD_DIM), F32),
                pltpu.VMEM((kv_rows, HEAD_DIM), F32),
                pltpu.VMEM((kv_rows, HEAD_DIM), F32)]
               + [pltpu.VMEM((t, HEAD_DIM), F32) for _ in range(3 * ng)])
    grid_spec = pltpu.PrefetchScalarGridSpec(
        num_scalar_prefetch=1,
        grid=(bsz, tiles, HEADS_PER_GROUP),
        in_specs=in_specs,
        out_specs=pl.BlockSpec((t, HEAD_DIM), lambda b, i, h, s: (b * tiles + i, h)),
        scratch_shapes=scratch,
    )
    return pl.pallas_call(
        _attn_body,
        grid_spec=grid_spec,
        out_shape=jax.ShapeDtypeStruct((m, HEADS_PER_GROUP * HEAD_DIM), BF16),
        compiler_params=_params("arbitrary", "arbitrary", "arbitrary"),
        name="attention",
    )(slopes, *([proj] * (5 * ng)))


def _outproj_body(ya_ref, yb_ref, ga_ref, gb_ref, x_ref, gt_ref, woa_ref, wob_ref, wout_ref,
                  o_ref):
    a = jnp.dot(ya_ref[...], woa_ref[...], preferred_element_type=F32)
    b = jnp.dot(yb_ref[...], wob_ref[...], preferred_element_type=F32)
    merged = (jax.nn.sigmoid(ga_ref[...].astype(F32)) * a
              + jax.nn.sigmoid(gb_ref[...].astype(F32)) * b)
    o = jnp.dot(merged.astype(BF16), wout_ref[...], preferred_element_type=F32)
    o_ref[...] = x_ref[...] + gt_ref[...] * o


def _out_proj(ya, yb, proj, xf, modr, w_oa, w_ob, w_out, layer, bsz, gate_col0, tm=512):
    m, d = xf.shape
    d_a = ya.shape[1]
    d_bo = yb.shape[1]
    tpb = (m // bsz) // tm
    gblk = gate_col0 // d
    const = lambda i: (0, 0)
    return pl.pallas_call(
        _outproj_body,
        grid=(m // tm,),
        in_specs=[
            pl.BlockSpec((tm, d_a), lambda i: (i, 0)),
            pl.BlockSpec((tm, d_bo), lambda i: (i, 0)),
            pl.BlockSpec((tm, d), lambda i: (i, gblk)),
            pl.BlockSpec((tm, d), lambda i: (i, gblk + 1)),
            pl.BlockSpec((tm, d), lambda i: (i, 0)),
            _mod_spec(d, layer, bsz, 2, tpb),
            pl.BlockSpec((d_a, d), const, pipeline_mode=pl.Buffered(1)),
            pl.BlockSpec((d_bo, d), const, pipeline_mode=pl.Buffered(1)),
            pl.BlockSpec((d, d), const, pipeline_mode=pl.Buffered(1)),
        ],
        out_specs=pl.BlockSpec((tm, d), lambda i: (i, 0)),
        out_shape=jax.ShapeDtypeStruct((m, d), F32),
        compiler_params=_params("arbitrary"),
        name="out_proj",
    )(ya, yb, proj, proj, xf, modr, w_oa, w_ob, w_out)


def _mlp_body(x_ref, g_ref, sh_ref, sc_ref, gt_ref, w1_ref, b1_ref, w2_ref, b2_ref, gf_ref,
              o_ref, h_ref, *, final_norm, rows=16):
    f = pl.program_id(1)

    @pl.when(f == 0)
    def _():
        _norm_modulate(h_ref, x_ref, g_ref, sh_ref, sc_ref)
        o_ref[...] = jnp.zeros_like(o_ref)

    hid = jnp.dot(h_ref[...], w1_ref[...], preferred_element_type=F32) + b1_ref[...]
    hid = jnp.square(jnp.maximum(hid, 0.0)).astype(BF16)
    o_ref[...] += jnp.dot(hid, w2_ref[...], preferred_element_type=F32)

    @pl.when(f == pl.num_programs(1) - 1)
    def _():
        gt = gt_ref[...]
        b2 = b2_ref[...]
        gf = gf_ref[...]

        def body(i, carry):
            r = pl.ds(pl.multiple_of(i * rows, rows), rows)
            y = x_ref[r, :] + gt * (o_ref[r, :] + b2)
            if final_norm:
                ms = jnp.mean(y * y, axis=-1, keepdims=True)
                y = y * lax.rsqrt(ms + EPS) * gf
            o_ref[r, :] = y
            return carry

        lax.fori_loop(0, x_ref.shape[0] // rows, body, 0, unroll=4)


def _mlp(xf, g, modr, w1, b1, w2, b2, g_final, layer, bsz, final_norm, tm=1024, tf=512):
    m, d = xf.shape
    dff = w1.shape[1]
    tpb = (m // bsz) // tm
    return pl.pallas_call(
        functools.partial(_mlp_body, final_norm=final_norm),
        grid=(m // tm, dff // tf),
        in_specs=[
            pl.BlockSpec((tm, d), lambda i, f: (i, 0), pipeline_mode=pl.Buffered(1)),
            pl.BlockSpec((1, d), lambda i, f: (0, 0)),
            _mod_spec(d, layer, bsz, 3, tpb),
            _mod_spec(d, layer, bsz, 4, tpb),
            _mod_spec(d, layer, bsz, 5, tpb),
            pl.BlockSpec((d, tf), lambda i, f: (0, f)),
            pl.BlockSpec((1, tf), lambda i, f: (0, f)),
            pl.BlockSpec((tf, d), lambda i, f: (f, 0)),
            pl.BlockSpec((1, d), lambda i, f: (0, 0)),
            pl.BlockSpec((1, d), lambda i, f: (0, 0)),
        ],
        out_specs=pl.BlockSpec((tm, d), lambda i, f: (i, 0)),
        out_shape=jax.ShapeDtypeStruct((m, d), F32),
        scratch_shapes=[pltpu.VMEM((tm, d), BF16)],
        compiler_params=_params("arbitrary", "arbitrary"),
        name="mlp",
    )(xf, g.reshape(1, d), modr, modr, modr, w1, b1.reshape(1, dff), w2, b2.reshape(1, d),
      g_final.reshape(1, d))


def kernel(x, c, w_ada, b_ada, g_mix, w_in, b_in, g_v, b_v, w_s, b_s, w_oa, w_ob, w_out,
           g_mlp, w1, b1, w2, b2, g_final):
    bsz, seq, d = x.shape
    depth = w_ada.shape[0]
    m = bsz * seq
    d_a = w_oa.shape[1]
    d_bo = w_ob.shape[1]
    d_b = d_bo * len(DILATIONS)
    assert seq % ATTN_TILE == 0 and d_bo == HEADS_PER_GROUP * HEAD_DIM

    qkv0, qkv1 = 2 * d_a, 2 * d_a + 3 * d_b

    def reorder(a):
        return jnp.concatenate([a[..., :qkv0], a[..., qkv1:], a[..., qkv0:qkv1]], axis=-1)

    gate_col0 = 2 * d_a
    qkv_col0 = 2 * d_a + 2 * d

    n_heads = d_b // HEAD_DIM
    slopes = 2.0 ** (-ALIBI_MAX_EXP * jnp.arange(1, n_heads + 1, dtype=F32) / n_heads)

    mod = _ada(c, w_ada, b_ada)
    modr = mod.reshape(depth * bsz * 6, 1, d)
    xf = x.reshape(m, d)
    for l in range(depth):
        w_in_l = reorder(w_in[l]).astype(BF16)
        b_in_l = reorder(b_in[l])
        proj = _in_proj(xf, g_mix[l], modr, w_in_l, b_in_l, l, bsz)
        ya = _gmlp(proj, g_v[l], b_v[l], w_s[l], b_s[l])
        yb = _attention(proj, slopes, bsz, qkv_col0, d_b)
        x1 = _out_proj(ya, yb, proj, xf, modr, w_oa[l].astype(BF16), w_ob[l].astype(BF16),
                       w_out[l].astype(BF16), l, bsz, gate_col0)
        xf = _mlp(x1, g_mlp[l], modr, w1[l].astype(BF16), b1[l], w2[l].astype(BF16), b2[l],
                  g_final, l, bsz, final_norm=(l == depth - 1))
    return xf.reshape(bsz, seq, d)
```

```python
import functools

import jax
import jax.numpy as jnp
from jax import lax
from jax.experimental import pallas as pl
from jax.experimental.pallas import tpu as pltpu

F32 = jnp.float32
BF16 = jnp.bfloat16

HEAD_DIM = 128
CHUNK = 128
Q_BLOCK = 128
SPAN = 128
DILATIONS = (1, 4, 16)
HEADS_PER_GROUP = 4
ALIBI_MAX_EXP = 8.0
EPS = 1e-6
ATTN_TILE = Q_BLOCK * max(DILATIONS)
NEG_INF = float("-inf")
LOG2E = 1.4426950408889634
FAST_STRIDE = 4
MIB = 1024 * 1024
VMEM_LIMIT = 56 * MIB


def _params(*sem, vmem=VMEM_LIMIT):
    return pltpu.CompilerParams(dimension_semantics=sem, vmem_limit_bytes=vmem)


def _ada_body(c_ref, w_ref, b_ref, o_ref):
    c = c_ref[...]
    act = (c * jax.nn.sigmoid(c)).astype(BF16)
    o_ref[...] = jnp.dot(act, w_ref[...].astype(BF16), preferred_element_type=F32) + b_ref[...]


def _ada(c, w_ada, b_ada):
    depth, d, n = w_ada.shape
    bsz = c.shape[0]
    tn = 1024
    return pl.pallas_call(
        _ada_body,
        grid=(depth, n // tn),
        in_specs=[
            pl.BlockSpec((bsz, d), lambda l, j: (0, 0)),
            pl.BlockSpec((None, d, tn), lambda l, j: (l, 0, j)),
            pl.BlockSpec((None, 1, tn), lambda l, j: (l, 0, j)),
        ],
        out_specs=pl.BlockSpec((None, bsz, tn), lambda l, j: (l, 0, j)),
        out_shape=jax.ShapeDtypeStruct((depth, bsz, n), F32),
        compiler_params=_params("arbitrary", "arbitrary"),
        name="ada",
    )(c, w_ada, b_ada.reshape(depth, 1, n))


def _norm_modulate(h_ref, x_ref, g_ref, sh_ref, sc_ref, rows=16, unroll=4):
    gs = g_ref[...] * (1.0 + sc_ref[...])
    sh = sh_ref[...]

    def body(i, carry):
        r0 = pl.multiple_of(i * rows, rows)
        xv = x_ref[pl.ds(r0, rows), :]
        ms = jnp.mean(xv * xv, axis=-1, keepdims=True)
        h_ref[pl.ds(r0, rows), :] = (xv * lax.rsqrt(ms + EPS) * gs + sh).astype(h_ref.dtype)
        return carry

    lax.fori_loop(0, x_ref.shape[0] // rows, body, 0, unroll=unroll)


def _mod_spec(d, layer, bsz, comp, tiles_per_batch):
    return pl.BlockSpec(
        (None, 1, d),
        lambda i, *_: ((layer * bsz + i // tiles_per_batch) * 6 + comp, 0, 0))


def _inproj_body(x_ref, g_ref, sh_ref, sc_ref, w_ref, b_ref, o_ref, h_ref):
    @pl.when(pl.program_id(1) == 0)
    def _():
        _norm_modulate(h_ref, x_ref, g_ref, sh_ref, sc_ref)

    acc = jnp.dot(h_ref[...], w_ref[...], preferred_element_type=F32)
    o_ref[...] = (acc + b_ref[...]).astype(o_ref.dtype)


def _in_proj(xf, g, modr, w, b, layer, bsz, tm=1024, tn=1536):
    m, d = xf.shape
    n = w.shape[1]
    tpb = (m // bsz) // tm
    return pl.pallas_call(
        _inproj_body,
        grid=(m // tm, n // tn),
        in_specs=[
            pl.BlockSpec((tm, d), lambda i, j: (i, 0)),
            pl.BlockSpec((1, d), lambda i, j: (0, 0)),
            _mod_spec(d, layer, bsz, 0, tpb),
            _mod_spec(d, layer, bsz, 1, tpb),
            pl.BlockSpec((d, tn), lambda i, j: (0, j)),
            pl.BlockSpec((1, tn), lambda i, j: (0, j)),
        ],
        out_specs=pl.BlockSpec((tm, tn), lambda i, j: (i, j)),
        out_shape=jax.ShapeDtypeStruct((m, n), BF16),
        scratch_shapes=[pltpu.VMEM((tm, d), BF16)],
        compiler_params=_params("arbitrary", "arbitrary"),
        name="in_proj",
    )(xf, g.reshape(1, d), modr, modr, w, b.reshape(1, n))


def _spatial_gating(u_ref, v_ref, gv_ref, bv_ref, ws_ref, bst_ref, o_ref):
    groups = ws_ref.shape[0]
    n_chunks = u_ref.shape[0] // CHUNK
    rowi = lax.broadcasted_iota(jnp.int32, (CHUNK, CHUNK), 0)
    coli = lax.broadcasted_iota(jnp.int32, (CHUNK, CHUNK), 1)
    causal = rowi >= coli
    gv = gv_ref[...]
    bv = bv_ref[...]
    ws = [jnp.where(causal, ws_ref[g], 0.0).astype(BF16) for g in range(groups)]
    for c in range(n_chunks):
        rs = slice(c * CHUNK, (c + 1) * CHUNK)
        v = jax.nn.gelu(v_ref[rs, :].astype(F32))
        mu = jnp.mean(v, axis=-1, keepdims=True)
        vc = v - mu
        var = jnp.mean(vc * vc, axis=-1, keepdims=True)
        vn = (vc * lax.rsqrt(var + EPS) * gv + bv).astype(BF16)
        for g in range(groups):
            cs = slice(g * HEAD_DIM, (g + 1) * HEAD_DIM)
            fv = jnp.dot(ws[g], vn[:, cs], preferred_element_type=F32) + bst_ref[:, g:g + 1]
            u = jax.nn.gelu(u_ref[rs, cs].astype(F32))
            o_ref[rs, cs] = (u * fv).astype(o_ref.dtype)


def _attn_body(slopes_ref, *refs):
    ng = len(DILATIONS)
    q_refs = refs[0:ng]
    kc_refs = refs[ng:2 * ng]
    kp_refs = refs[2 * ng:3 * ng]
    vc_refs = refs[3 * ng:4 * ng]
    vp_refs = refs[4 * ng:5 * ng]
    o_ref = refs[5 * ng]
    qbuf, kbuf, vbuf, qbuf2, kbuf2, vbuf2 = refs[5 * ng + 1:5 * ng + 7]
    u_scr = refs[5 * ng + 7:5 * ng + 7 + ng]
    m_scr = refs[5 * ng + 7 + ng:5 * ng + 7 + 2 * ng]
    l_scr = refs[5 * ng + 7 + 2 * ng:5 * ng + 7 + 3 * ng]

    tile = pl.program_id(1)
    hg = pl.program_id(2)
    t = ATTN_TILE
    q_scale = HEAD_DIM ** -0.5 * LOG2E

    qi = lax.broadcasted_iota(jnp.int32, (Q_BLOCK, 2 * Q_BLOCK), 0) + Q_BLOCK
    ki = lax.broadcasted_iota(jnp.int32, (Q_BLOCK, 2 * Q_BLOCK), 1)
    dist = qi - ki
    in_band = (dist >= 0) & (dist <= SPAN)
    before_start = ki < Q_BLOCK

    def attend(qb, kb, vb, bias):
        s = lax.dot_general(qb, kb, (((1,), (1,)), ((), ())),
                            preferred_element_type=F32) + bias
        mx = jnp.max(s, axis=-1, keepdims=True)
        p = jnp.exp2(s - mx)
        den = jnp.sum(p, axis=-1, keepdims=True)
        u = jnp.dot(p.astype(BF16), vb, preferred_element_type=F32)
        return u, mx, den

    for g, d in enumerate(DILATIONS):
        nblk = t // (Q_BLOCK * d)
        back = Q_BLOCK * d
        q_ref, kc_ref, kp_ref, vc_ref, vp_ref = (q_refs[g], kc_refs[g], kp_refs[g],
                                                 vc_refs[g], vp_refs[g])
        slope = slopes_ref[g * HEADS_PER_GROUP + hg] * LOG2E
        bias = jnp.where(in_band, -slope * (d * dist).astype(F32), NEG_INF)
        bias_first = jnp.where(jnp.logical_and(tile == 0, before_start), NEG_INF, bias)

        if d > 1:
            rows_kv = back + t
            qbuf[...] = q_ref[...].astype(F32) * q_scale
            kbuf[0:back, :] = kp_ref[t - back:t, :].astype(F32)
            kbuf[back:rows_kv, :] = kc_ref[...].astype(F32)
            vbuf[0:back, :] = vp_ref[t - back:t, :].astype(F32)
            vbuf[back:rows_kv, :] = vc_ref[...].astype(F32)
        if d > FAST_STRIDE:
            for j in range(FAST_STRIDE):
                nq, nkv = t // FAST_STRIDE, rows_kv // FAST_STRIDE
                qbuf2[j * nq:(j + 1) * nq, :] = qbuf[pl.ds(j, nq, stride=FAST_STRIDE), :]
                kbuf2[j * nkv:(j + 1) * nkv, :] = kbuf[pl.ds(j, nkv, stride=FAST_STRIDE), :]
                vbuf2[j * nkv:(j + 1) * nkv, :] = vbuf[pl.ds(j, nkv, stride=FAST_STRIDE), :]

        for r in range(d):
            for n in range(nblk):
                if d == 1:
                    lo = n * Q_BLOCK
                    out_rows = pl.ds(lo, Q_BLOCK)
                    qb = (q_ref[out_rows, :].astype(F32) * q_scale).astype(BF16)
                    if n == 0:
                        kb = jnp.concatenate([kp_ref[t - Q_BLOCK:t, :], kc_ref[0:Q_BLOCK, :]], 0)
                        vb = jnp.concatenate([vp_ref[t - Q_BLOCK:t, :], vc_ref[0:Q_BLOCK, :]], 0)
                    else:
                        kb = kc_ref[lo - Q_BLOCK:lo + Q_BLOCK, :]
                        vb = vc_ref[lo - Q_BLOCK:lo + Q_BLOCK, :]
                else:
                    out_rows = pl.ds(r + d * Q_BLOCK * n, Q_BLOCK, stride=d)
                    if d > FAST_STRIDE:
                        j, i, d2 = r % FAST_STRIDE, r // FAST_STRIDE, d // FAST_STRIDE
                        q_lo = j * (t // FAST_STRIDE) + i + d2 * Q_BLOCK * n
                        kv_lo = j * (rows_kv // FAST_STRIDE) + i + d2 * Q_BLOCK * n
                        src = (qbuf2, kbuf2, vbuf2)
                    else:
                        d2 = d
                        q_lo = kv_lo = r + d * Q_BLOCK * n
                        src = (qbuf, kbuf, vbuf)
                    qb = src[0][pl.ds(q_lo, Q_BLOCK, stride=d2), :].astype(BF16)
                    kb = src[1][pl.ds(kv_lo, 2 * Q_BLOCK, stride=d2), :].astype(BF16)
                    vb = src[2][pl.ds(kv_lo, 2 * Q_BLOCK, stride=d2), :].astype(BF16)
                u, mx, den = attend(qb, kb, vb, bias_first if n == 0 else bias)
                u_scr[g][out_rows, :] = u
                m_scr[g][out_rows, :] = jnp.broadcast_to(mx, (Q_BLOCK, HEAD_DIM))
                l_scr[g][out_rows, :] = jnp.broadcast_to(den, (Q_BLOCK, HEAD_DIM))

    def combine(i, carry):
        rows = pl.ds(pl.multiple_of(i * Q_BLOCK, Q_BLOCK), Q_BLOCK)
        ms = [m_scr[g][rows, :] for g in range(ng)]
        top = functools.reduce(jnp.maximum, ms)
        num = jnp.zeros((Q_BLOCK, HEAD_DIM), F32)
        den = jnp.zeros((Q_BLOCK, HEAD_DIM), F32)
        for g in range(ng):
            w = jnp.exp2(ms[g] - top)
            num = num + w * u_scr[g][rows, :]
            den = den + w * l_scr[g][rows, :]
        o_ref[rows, :] = (num / den).astype(o_ref.dtype)
        return carry

    lax.fori_loop(0, t // Q_BLOCK, combine, 0, unroll=2)


def _attention(proj, slopes, bsz, col0, d_b):
    m = proj.shape[0]
    t = ATTN_TILE
    tiles = (m // bsz) // t
    ng = len(DILATIONS)
    hb = col0 // HEAD_DIM
    heads = d_b // HEAD_DIM

    def cur(base, g):
        return pl.BlockSpec(
            (t, HEAD_DIM),
            lambda b, i, h, s: (b * tiles + i, base + g * HEADS_PER_GROUP + h))

    def prev(base, g):
        return pl.BlockSpec(
            (t, HEAD_DIM),
            lambda b, i, h, s: (b * tiles + jnp.maximum(i - 1, 0),
                                base + g * HEADS_PER_GROUP + h))

    in_specs = ([cur(hb, g) for g in range(ng)]
                + [cur(hb + heads, g) for g in range(ng)]
                + [prev(hb + heads, g) for g in range(ng)]
                + [cur(hb + 2 * heads, g) for g in range(ng)]
                + [prev(hb + 2 * heads, g) for g in range(ng)])
    kv_rows = t + Q_BLOCK * max(DILATIONS)
    staging = [pltpu.VMEM((t, HEAD_DIM), F32),
               pltpu.VMEM((kv_rows, HEAD_DIM), F32),
               pltpu.VMEM((kv_rows, HEAD_DIM), F32)]
    scratch = staging + staging + [pltpu.VMEM((t, HEAD_DIM), F32) for _ in range(3 * ng)]
    grid_spec = pltpu.PrefetchScalarGridSpec(
        num_scalar_prefetch=1,
        grid=(bsz, tiles, HEADS_PER_GROUP),
        in_specs=in_specs,
        out_specs=pl.BlockSpec((t, HEAD_DIM), lambda b, i, h, s: (b * tiles + i, h)),
        scratch_shapes=scratch,
    )
    return pl.pallas_call(
        _attn_body,
        grid_spec=grid_spec,
        out_shape=jax.ShapeDtypeStruct((m, HEADS_PER_GROUP * HEAD_DIM), BF16),
        compiler_params=_params("arbitrary", "arbitrary", "arbitrary"),
        name="attention",
    )(slopes, *([proj] * (5 * ng)))


def _outproj_body(u_ref, v_ref, gv_ref, bv_ref, ws_ref, bst_ref, yb_ref, ga_ref, gb_ref, x_ref,
                  gt_ref, woa_ref, wob_ref, wout_ref, o_ref, ya_ref):
    _spatial_gating(u_ref, v_ref, gv_ref, bv_ref, ws_ref, bst_ref, ya_ref)
    a = jnp.dot(ya_ref[...], woa_ref[...], preferred_element_type=F32)
    b = jnp.dot(yb_ref[...], wob_ref[...], preferred_element_type=F32)
    merged = (jax.nn.sigmoid(ga_ref[...].astype(F32)) * a
              + jax.nn.sigmoid(gb_ref[...].astype(F32)) * b)
    o = jnp.dot(merged.astype(BF16), wout_ref[...], preferred_element_type=F32)
    o_ref[...] = x_ref[...] + gt_ref[...] * o


def _out_proj(yb, proj, xf, modr, g_v, b_v, w_s, b_s, w_oa, w_ob, w_out, layer, bsz, gate_col0,
              tm=512):
    m, d = xf.shape
    d_a = w_oa.shape[0]
    d_bo = yb.shape[1]
    groups = w_s.shape[0]
    tpb = (m // bsz) // tm
    gblk = gate_col0 // d
    const = lambda i: (0, 0)
    return pl.pallas_call(
        _outproj_body,
        grid=(m // tm,),
        in_specs=[
            pl.BlockSpec((tm, d_a), lambda i: (i, 0)),
            pl.BlockSpec((tm, d_a), lambda i: (i, 1)),
            pl.BlockSpec((1, d_a), const),
            pl.BlockSpec((1, d_a), const),
            pl.BlockSpec((groups, CHUNK, CHUNK), lambda i: (0, 0, 0)),
            pl.BlockSpec((CHUNK, groups), const),
            pl.BlockSpec((tm, d_bo), lambda i: (i, 0)),
            pl.BlockSpec((tm, d), lambda i: (i, gblk)),
            pl.BlockSpec((tm, d), lambda i: (i, gblk + 1)),
            pl.BlockSpec((tm, d), lambda i: (i, 0)),
            _mod_spec(d, layer, bsz, 2, tpb),
            pl.BlockSpec((d_a, d), const, pipeline_mode=pl.Buffered(1)),
            pl.BlockSpec((d_bo, d), const, pipeline_mode=pl.Buffered(1)),
            pl.BlockSpec((d, d), const, pipeline_mode=pl.Buffered(1)),
        ],
        out_specs=pl.BlockSpec((tm, d), lambda i: (i, 0)),
        out_shape=jax.ShapeDtypeStruct((m, d), F32),
        scratch_shapes=[pltpu.VMEM((tm, d_a), BF16)],
        compiler_params=_params("arbitrary", vmem=60 * MIB),
        name="out_proj",
    )(proj, proj, g_v.reshape(1, d_a), b_v.reshape(1, d_a), w_s, b_s.T, yb, proj, proj, xf,
      modr, w_oa, w_ob, w_out)


def _mlp_body(x_ref, g_ref, sh_ref, sc_ref, gt_ref, w1_ref, b1_ref, w2_ref, b2_ref, gf_ref,
              o_ref, h_ref, *, final_norm, rows=16):
    f = pl.program_id(1)

    @pl.when(f == 0)
    def _():
        _norm_modulate(h_ref, x_ref, g_ref, sh_ref, sc_ref)
        o_ref[...] = jnp.zeros_like(o_ref)

    hid = jnp.dot(h_ref[...], w1_ref[...], preferred_element_type=F32) + b1_ref[...]
    hid = jnp.square(jnp.maximum(hid, 0.0)).astype(BF16)
    o_ref[...] += jnp.dot(hid, w2_ref[...], preferred_element_type=F32)

    @pl.when(f == pl.num_programs(1) - 1)
    def _():
        gt = gt_ref[...]
        b2 = b2_ref[...]
        gf = gf_ref[...]

        def body(i, carry):
            r = pl.ds(pl.multiple_of(i * rows, rows), rows)
            y = x_ref[r, :] + gt * (o_ref[r, :] + b2)
            if final_norm:
                ms = jnp.mean(y * y, axis=-1, keepdims=True)
                y = y * lax.rsqrt(ms + EPS) * gf
            o_ref[r, :] = y
            return carry

        lax.fori_loop(0, x_ref.shape[0] // rows, body, 0, unroll=4)


def _mlp(xf, g, modr, w1, b1, w2, b2, g_final, layer, bsz, final_norm, tm=1024, tf=1024):
    m, d = xf.shape
    dff = w1.shape[1]
    tpb = (m // bsz) // tm
    return pl.pallas_call(
        functools.partial(_mlp_body, final_norm=final_norm),
        grid=(m // tm, dff // tf),
        in_specs=[
            pl.BlockSpec((tm, d), lambda i, f: (i, 0)),
            pl.BlockSpec((1, d), lambda i, f: (0, 0)),
            _mod_spec(d, layer, bsz, 3, tpb),
            _mod_spec(d, layer, bsz, 4, tpb),
            _mod_spec(d, layer, bsz, 5, tpb),
            pl.BlockSpec((d, tf), lambda i, f: (0, f)),
            pl.BlockSpec((1, tf), lambda i, f: (0, f)),
            pl.BlockSpec((tf, d), lambda i, f: (f, 0)),
            pl.BlockSpec((1, d), lambda i, f: (0, 0)),
            pl.BlockSpec((1, d), lambda i, f: (0, 0)),
        ],
        out_specs=pl.BlockSpec((tm, d), lambda i, f: (i, 0)),
        out_shape=jax.ShapeDtypeStruct((m, d), F32),
        scratch_shapes=[pltpu.VMEM((tm, d), BF16)],
        compiler_params=_params("arbitrary", "arbitrary", vmem=60 * MIB),
        name="mlp",
    )(xf, g.reshape(1, d), modr, modr, modr, w1, b1.reshape(1, dff), w2, b2.reshape(1, d),
      g_final.reshape(1, d))


def kernel(x, c, w_ada, b_ada, g_mix, w_in, b_in, g_v, b_v, w_s, b_s, w_oa, w_ob, w_out,
           g_mlp, w1, b1, w2, b2, g_final):
    bsz, seq, d = x.shape
    depth = w_ada.shape[0]
    m = bsz * seq
    d_a = w_oa.shape[1]
    d_bo = w_ob.shape[1]
    d_b = d_bo * len(DILATIONS)
    assert seq % ATTN_TILE == 0 and d_bo == HEADS_PER_GROUP * HEAD_DIM

    qkv0, qkv1 = 2 * d_a, 2 * d_a + 3 * d_b

    def reorder(a):
        return jnp.concatenate([a[..., :qkv0], a[..., qkv1:], a[..., qkv0:qkv1]], axis=-1)

    gate_col0 = 2 * d_a
    qkv_col0 = 2 * d_a + 2 * d

    n_heads = d_b // HEAD_DIM
    slopes = 2.0 ** (-ALIBI_MAX_EXP * jnp.arange(1, n_heads + 1, dtype=F32) / n_heads)

    mod = _ada(c, w_ada, b_ada)
    modr = mod.reshape(depth * bsz * 6, 1, d)
    xf = x.reshape(m, d)
    for l in range(depth):
        w_in_l = reorder(w_in[l]).astype(BF16)
        b_in_l = reorder(b_in[l])
        proj = _in_proj(xf, g_mix[l], modr, w_in_l, b_in_l, l, bsz)
        yb = _attention(proj, slopes, bsz, qkv_col0, d_b)
        x1 = _out_proj(yb, proj, xf, modr, g_v[l], b_v[l], w_s[l], b_s[l], w_oa[l].astype(BF16),
                       w_ob[l].astype(BF16), w_out[l].astype(BF16), l, bsz, gate_col0)
        xf = _mlp(x1, g_mlp[l], modr, w1[l].astype(BF16), b1[l], w2[l].astype(BF16), b2[l],
                  g_final, l, bsz, final_norm=(l == depth - 1))
    return xf.reshape(bsz, seq, d)
```

```python
import functools
import math

import jax
import jax.numpy as jnp
from jax import lax
from jax.experimental import pallas as pl
from jax.experimental.pallas import tpu as pltpu

F32 = jnp.float32
BF16 = jnp.bfloat16

HEAD_DIM = 128
CHUNK = 128
Q_BLOCK = 128
SPAN = 128
DILATIONS = (1, 4, 16)
HEADS_PER_GROUP = 4
ALIBI_MAX_EXP = 8.0
EPS = 1e-6
ATTN_TILE = Q_BLOCK * max(DILATIONS)
NEG_INF = float("-inf")
LOG2E = 1.4426950408889634
FAST_STRIDE = 4
LANES = 128
MIB = 1024 * 1024
VMEM_LIMIT = 56 * MIB


def _params(*sem, vmem=VMEM_LIMIT):
    return pltpu.CompilerParams(dimension_semantics=sem, vmem_limit_bytes=vmem)


def _ada_body(c_ref, w_ref, b_ref, o_ref):
    c = c_ref[...]
    act = (c * jax.nn.sigmoid(c)).astype(BF16)
    o_ref[...] = jnp.dot(act, w_ref[...].astype(BF16), preferred_element_type=F32) + b_ref[...]


def _ada(c, w_ada, b_ada):
    depth, d, n = w_ada.shape
    bsz = c.shape[0]
    tn = 1024
    return pl.pallas_call(
        _ada_body,
        grid=(depth, n // tn),
        in_specs=[
            pl.BlockSpec((bsz, d), lambda l, j: (0, 0)),
            pl.BlockSpec((None, d, tn), lambda l, j: (l, 0, j)),
            pl.BlockSpec((None, 1, tn), lambda l, j: (l, 0, j)),
        ],
        out_specs=pl.BlockSpec((None, bsz, tn), lambda l, j: (l, 0, j)),
        out_shape=jax.ShapeDtypeStruct((depth, bsz, n), F32),
        compiler_params=_params("arbitrary", "arbitrary"),
        name="ada",
    )(c, w_ada, b_ada.reshape(depth, 1, n))


def _norm_modulate(h_ref, x_ref, g_ref, sh_ref, sc_ref, rows=16, unroll=4):
    gs = g_ref[...] * (1.0 + sc_ref[...])
    sh = sh_ref[...]

    def body(i, carry):
        r0 = pl.multiple_of(i * rows, rows)
        xv = x_ref[pl.ds(r0, rows), :]
        ms = jnp.mean(xv * xv, axis=-1, keepdims=True)
        h_ref[pl.ds(r0, rows), :] = (xv * lax.rsqrt(ms + EPS) * gs + sh).astype(h_ref.dtype)
        return carry

    lax.fori_loop(0, x_ref.shape[0] // rows, body, 0, unroll=unroll)


def _mod_spec(d, layer, bsz, comp, tiles_per_batch):
    return pl.BlockSpec(
        (None, 1, d),
        lambda i, *_: ((layer * bsz + i // tiles_per_batch) * 6 + comp, 0, 0))


def _inproj_body(x_ref, g_ref, sh_ref, sc_ref, w_ref, b_ref, o_ref, h_ref):
    @pl.when(pl.program_id(1) == 0)
    def _():
        _norm_modulate(h_ref, x_ref, g_ref, sh_ref, sc_ref)

    acc = jnp.dot(h_ref[...], w_ref[...], preferred_element_type=F32)
    o_ref[...] = (acc + b_ref[...]).astype(o_ref.dtype)


def _in_proj(xf, g, modr, w, b, layer, bsz, tm=1024, tn=1536):
    m, d = xf.shape
    n = w.shape[2]
    tpb = (m // bsz) // tm
    return pl.pallas_call(
        _inproj_body,
        grid=(m // tm, n // tn),
        in_specs=[
            pl.BlockSpec((tm, d), lambda i, j: (i, 0)),
            pl.BlockSpec((None, 1, d), lambda i, j: (layer, 0, 0)),
            _mod_spec(d, layer, bsz, 0, tpb),
            _mod_spec(d, layer, bsz, 1, tpb),
            pl.BlockSpec((None, d, tn), lambda i, j: (layer, 0, j)),
            pl.BlockSpec((None, 1, tn), lambda i, j: (layer, 0, j)),
        ],
        out_specs=pl.BlockSpec((tm, tn), lambda i, j: (i, j)),
        out_shape=jax.ShapeDtypeStruct((m, n), BF16),
        scratch_shapes=[pltpu.VMEM((tm, d), BF16)],
        compiler_params=_params("arbitrary", "arbitrary"),
        name="in_proj",
    )(xf, g, modr, modr, w, b)


def _spatial_gating(u_ref, v_ref, gv_ref, bv_ref, ws_ref, bst_ref, o_ref):
    groups = ws_ref.shape[0]
    n_chunks = u_ref.shape[0] // CHUNK
    rowi = lax.broadcasted_iota(jnp.int32, (CHUNK, CHUNK), 0)
    coli = lax.broadcasted_iota(jnp.int32, (CHUNK, CHUNK), 1)
    causal = rowi >= coli
    gv = gv_ref[...]
    bv = bv_ref[...]
    ws = [jnp.where(causal, ws_ref[g], 0.0).astype(BF16) for g in range(groups)]
    for c in range(n_chunks):
        rs = slice(c * CHUNK, (c + 1) * CHUNK)
        v = jax.nn.gelu(v_ref[rs, :].astype(F32))
        mu = jnp.mean(v, axis=-1, keepdims=True)
        vc = v - mu
        var = jnp.mean(vc * vc, axis=-1, keepdims=True)
        vn = (vc * lax.rsqrt(var + EPS) * gv + bv).astype(BF16)
        for g in range(groups):
            cs = slice(g * HEAD_DIM, (g + 1) * HEAD_DIM)
            fv = jnp.dot(ws[g], vn[:, cs], preferred_element_type=F32) + bst_ref[:, g:g + 1]
            u = jax.nn.gelu(u_ref[rs, cs].astype(F32))
            o_ref[rs, cs] = (u * fv).astype(o_ref.dtype)


def _attn_body(slopes_ref, *refs):
    ng = len(DILATIONS)
    q_refs = refs[0:ng]
    kc_refs = refs[ng:2 * ng]
    kp_refs = refs[2 * ng:3 * ng]
    vc_refs = refs[3 * ng:4 * ng]
    vp_refs = refs[4 * ng:5 * ng]
    o_ref = refs[5 * ng]
    qbuf, kbuf, vbuf, qbuf2, kbuf2, vbuf2 = refs[5 * ng + 1:5 * ng + 7]
    u_scr = refs[5 * ng + 7:5 * ng + 7 + ng]
    m_scr = refs[5 * ng + 7 + ng:5 * ng + 7 + 2 * ng]
    l_scr = refs[5 * ng + 7 + 2 * ng:5 * ng + 7 + 3 * ng]

    tile = pl.program_id(1)
    hg = pl.program_id(2)
    t = ATTN_TILE
    q_scale = HEAD_DIM ** -0.5 * LOG2E

    qi = lax.broadcasted_iota(jnp.int32, (Q_BLOCK, 2 * Q_BLOCK), 0) + Q_BLOCK
    ki = lax.broadcasted_iota(jnp.int32, (Q_BLOCK, 2 * Q_BLOCK), 1)
    dist = qi - ki
    in_band = (dist >= 0) & (dist <= SPAN)
    before_start = ki < Q_BLOCK

    def attend(qb, kb, vb, bias):
        s = lax.dot_general(qb, kb, (((1,), (1,)), ((), ())),
                            preferred_element_type=F32) + bias
        mx = jnp.max(s, axis=-1, keepdims=True)
        p = jnp.exp2(s - mx)
        den = jnp.sum(p, axis=-1, keepdims=True)
        u = jnp.dot(p.astype(BF16), vb, preferred_element_type=F32)
        return u, mx, den

    for g, d in enumerate(DILATIONS):
        nblk = t // (Q_BLOCK * d)
        back = Q_BLOCK * d
        q_ref, kc_ref, kp_ref, vc_ref, vp_ref = (q_refs[g], kc_refs[g], kp_refs[g],
                                                 vc_refs[g], vp_refs[g])
        slope = slopes_ref[g * HEADS_PER_GROUP + hg] * LOG2E
        bias = jnp.where(in_band, -slope * (d * dist).astype(F32), NEG_INF)
        bias_first = jnp.where(jnp.logical_and(tile == 0, before_start), NEG_INF, bias)

        if d > 1:
            rows_kv = back + t
            qbuf[...] = q_ref[...].astype(F32) * q_scale
            kbuf[0:back, :] = kp_ref[t - back:t, :].astype(F32)
            kbuf[back:rows_kv, :] = kc_ref[...].astype(F32)
            vbuf[0:back, :] = vp_ref[t - back:t, :].astype(F32)
            vbuf[back:rows_kv, :] = vc_ref[...].astype(F32)
        if d > FAST_STRIDE:
            for j in range(FAST_STRIDE):
                nq, nkv = t // FAST_STRIDE, rows_kv // FAST_STRIDE
                qbuf2[j * nq:(j + 1) * nq, :] = qbuf[pl.ds(j, nq, stride=FAST_STRIDE), :]
                kbuf2[j * nkv:(j + 1) * nkv, :] = kbuf[pl.ds(j, nkv, stride=FAST_STRIDE), :]
                vbuf2[j * nkv:(j + 1) * nkv, :] = vbuf[pl.ds(j, nkv, stride=FAST_STRIDE), :]

        for r in range(d):
            for n in range(nblk):
                if d == 1:
                    lo = n * Q_BLOCK
                    out_rows = pl.ds(lo, Q_BLOCK)
                    qb = (q_ref[out_rows, :].astype(F32) * q_scale).astype(BF16)
                    if n == 0:
                        kb = jnp.concatenate([kp_ref[t - Q_BLOCK:t, :], kc_ref[0:Q_BLOCK, :]], 0)
                        vb = jnp.concatenate([vp_ref[t - Q_BLOCK:t, :], vc_ref[0:Q_BLOCK, :]], 0)
                    else:
                        kb = kc_ref[lo - Q_BLOCK:lo + Q_BLOCK, :]
                        vb = vc_ref[lo - Q_BLOCK:lo + Q_BLOCK, :]
                else:
                    out_rows = pl.ds(r + d * Q_BLOCK * n, Q_BLOCK, stride=d)
                    if d > FAST_STRIDE:
                        j, i, d2 = r % FAST_STRIDE, r // FAST_STRIDE, d // FAST_STRIDE
                        q_lo = j * (t // FAST_STRIDE) + i + d2 * Q_BLOCK * n
                        kv_lo = j * (rows_kv // FAST_STRIDE) + i + d2 * Q_BLOCK * n
                        src = (qbuf2, kbuf2, vbuf2)
                    else:
                        d2 = d
                        q_lo = kv_lo = r + d * Q_BLOCK * n
                        src = (qbuf, kbuf, vbuf)
                    qb = src[0][pl.ds(q_lo, Q_BLOCK, stride=d2), :].astype(BF16)
                    kb = src[1][pl.ds(kv_lo, 2 * Q_BLOCK, stride=d2), :].astype(BF16)
                    vb = src[2][pl.ds(kv_lo, 2 * Q_BLOCK, stride=d2), :].astype(BF16)
                u, mx, den = attend(qb, kb, vb, bias_first if n == 0 else bias)
                u_scr[g][out_rows, :] = u
                m_scr[g][out_rows, :] = jnp.broadcast_to(mx, (Q_BLOCK, HEAD_DIM))
                l_scr[g][out_rows, :] = jnp.broadcast_to(den, (Q_BLOCK, HEAD_DIM))

    def combine(i, carry):
        rows = pl.ds(pl.multiple_of(i * Q_BLOCK, Q_BLOCK), Q_BLOCK)
        ms = [m_scr[g][rows, :] for g in range(ng)]
        top = functools.reduce(jnp.maximum, ms)
        num = jnp.zeros((Q_BLOCK, HEAD_DIM), F32)
        den = jnp.zeros((Q_BLOCK, HEAD_DIM), F32)
        for g in range(ng):
            w = jnp.exp2(ms[g] - top)
            num = num + w * u_scr[g][rows, :]
            den = den + w * l_scr[g][rows, :]
        o_ref[rows, :] = (num / den).astype(o_ref.dtype)
        return carry

    lax.fori_loop(0, t // Q_BLOCK, combine, 0, unroll=2)


def _attention(proj, slopes, bsz, col0, d_b):
    m = proj.shape[0]
    t = ATTN_TILE
    tiles = (m // bsz) // t
    ng = len(DILATIONS)
    hb = col0 // HEAD_DIM
    heads = d_b // HEAD_DIM

    def cur(base, g):
        return pl.BlockSpec(
            (t, HEAD_DIM),
            lambda b, i, h, s: (b * tiles + i, base + g * HEADS_PER_GROUP + h))

    def prev(base, g):
        return pl.BlockSpec(
            (t, HEAD_DIM),
            lambda b, i, h, s: (b * tiles + jnp.maximum(i - 1, 0),
                                base + g * HEADS_PER_GROUP + h))

    in_specs = ([cur(hb, g) for g in range(ng)]
                + [cur(hb + heads, g) for g in range(ng)]
                + [prev(hb + heads, g) for g in range(ng)]
                + [cur(hb + 2 * heads, g) for g in range(ng)]
                + [prev(hb + 2 * heads, g) for g in range(ng)])
    kv_rows = t + Q_BLOCK * max(DILATIONS)
    staging = [pltpu.VMEM((t, HEAD_DIM), F32),
               pltpu.VMEM((kv_rows, HEAD_DIM), F32),
               pltpu.VMEM((kv_rows, HEAD_DIM), F32)]
    scratch = staging + staging + [pltpu.VMEM((t, HEAD_DIM), F32) for _ in range(3 * ng)]
    grid_spec = pltpu.PrefetchScalarGridSpec(
        num_scalar_prefetch=1,
        grid=(bsz, tiles, HEADS_PER_GROUP),
        in_specs=in_specs,
        out_specs=pl.BlockSpec((t, HEAD_DIM), lambda b, i, h, s: (b * tiles + i, h)),
        scratch_shapes=scratch,
    )
    return pl.pallas_call(
        _attn_body,
        grid_spec=grid_spec,
        out_shape=jax.ShapeDtypeStruct((m, HEADS_PER_GROUP * HEAD_DIM), BF16),
        compiler_params=_params("arbitrary", "arbitrary", "arbitrary"),
        name="attention",
    )(slopes, *([proj] * (5 * ng)))


def _outproj_body(u_ref, v_ref, gv_ref, bv_ref, ws_ref, bst_ref, yb_ref, *refs, gate_pieces):
    ga_refs = refs[:gate_pieces]
    gb_refs = refs[gate_pieces:2 * gate_pieces]
    x_ref, gt_ref, woa_ref, wob_ref, wout_ref, o_ref, ya_ref = refs[2 * gate_pieces:]
    _spatial_gating(u_ref, v_ref, gv_ref, bv_ref, ws_ref, bst_ref, ya_ref)
    a = jnp.dot(ya_ref[...], woa_ref[...], preferred_element_type=F32)
    b = jnp.dot(yb_ref[...], wob_ref[...], preferred_element_type=F32)
    gw = ga_refs[0].shape[1]
    merged = jnp.concatenate(
        [(jax.nn.sigmoid(ga_refs[k][...].astype(F32)) * a[:, k * gw:(k + 1) * gw]
          + jax.nn.sigmoid(gb_refs[k][...].astype(F32)) * b[:, k * gw:(k + 1) * gw]).astype(BF16)
         for k in range(gate_pieces)], axis=1)
    o = jnp.dot(merged, wout_ref[...], preferred_element_type=F32)
    o_ref[...] = x_ref[...] + gt_ref[...] * o


def _out_proj(yb, proj, xf, modr, g_v, b_v, w_s, b_st, w_oa, w_ob, w_out, layer, bsz, gate_col0,
              tm=512):
    m, d = xf.shape
    d_a = w_oa.shape[1]
    d_bo = yb.shape[1]
    groups = w_s.shape[1]
    tpb = (m // bsz) // tm
    gw = math.gcd(gate_col0, d)
    pieces = d // gw
    lay3 = lambda i: (layer, 0, 0)

    def gate(col0):
        return [pl.BlockSpec((tm, gw), lambda i, c=col0 // gw + k: (i, c)) for k in range(pieces)]

    return pl.pallas_call(
        functools.partial(_outproj_body, gate_pieces=pieces),
        grid=(m // tm,),
        in_specs=[
            pl.BlockSpec((tm, d_a), lambda i: (i, 0)),
            pl.BlockSpec((tm, d_a), lambda i: (i, 1)),
            pl.BlockSpec((None, 1, d_a), lay3),
            pl.BlockSpec((None, 1, d_a), lay3),
            pl.BlockSpec((None, groups, CHUNK, CHUNK), lambda i: (layer, 0, 0, 0)),
            pl.BlockSpec((None, CHUNK, groups), lay3),
            pl.BlockSpec((tm, d_bo), lambda i: (i, 0)),
            *gate(gate_col0),
            *gate(gate_col0 + d),
            pl.BlockSpec((tm, d), lambda i: (i, 0)),
            _mod_spec(d, layer, bsz, 2, tpb),
            pl.BlockSpec((None, d_a, d), lay3, pipeline_mode=pl.Buffered(1)),
            pl.BlockSpec((None, d_bo, d), lay3, pipeline_mode=pl.Buffered(1)),
            pl.BlockSpec((None, d, d), lay3, pipeline_mode=pl.Buffered(1)),
        ],
        out_specs=pl.BlockSpec((tm, d), lambda i: (i, 0)),
        out_shape=jax.ShapeDtypeStruct((m, d), F32),
        scratch_shapes=[pltpu.VMEM((tm, d_a), BF16)],
        compiler_params=_params("arbitrary", vmem=60 * MIB),
        name="out_proj",
    )(proj, proj, g_v, b_v, w_s, b_st, yb, *([proj] * (2 * pieces)), xf, modr, w_oa, w_ob, w_out)


def _mlp_body(x_ref, g_ref, sh_ref, sc_ref, gt_ref, w1_ref, b1_ref, w2_ref, b2_ref, gf_ref,
              o_ref, h_ref, inv_ref, *, final_norm, rows=16):
    f = pl.program_id(1)
    n_row_steps = x_ref.shape[0] // rows

    @pl.when(f == 0)
    def _():
        _norm_modulate(h_ref, x_ref, g_ref, sh_ref, sc_ref)
        o_ref[...] = jnp.zeros_like(o_ref)

    hid = jnp.dot(h_ref[...], w1_ref[...], preferred_element_type=F32) + b1_ref[...]
    hid = jnp.square(jnp.maximum(hid, 0.0)).astype(BF16)
    o_ref[...] += jnp.dot(hid, w2_ref[...], preferred_element_type=F32)

    @pl.when(f == pl.num_programs(1) - 1)
    def _():
        gt = gt_ref[...]
        b2 = b2_ref[...]

        def residual(i, carry):
            r = pl.ds(pl.multiple_of(i * rows, rows), rows)
            y = x_ref[r, :] + gt * (o_ref[r, :] + b2)
            o_ref[r, :] = y
            if final_norm:
                ms = jnp.mean(y * y, axis=-1, keepdims=True)
                inv_ref[r, :] = jnp.broadcast_to(lax.rsqrt(ms + EPS), (rows, inv_ref.shape[1]))
            return carry

        lax.fori_loop(0, n_row_steps, residual, 0, unroll=4)

        if final_norm:
            gf = gf_ref[...]
            reps = o_ref.shape[1] // inv_ref.shape[1]

            def normalise(i, carry):
                r = pl.ds(pl.multiple_of(i * rows, rows), rows)
                inv = jnp.concatenate([inv_ref[r, :]] * reps, axis=1)
                o_ref[r, :] = o_ref[r, :] * inv * gf
                return carry

            lax.fori_loop(0, n_row_steps, normalise, 0, unroll=4)


def _mlp(xf, g, modr, w1, b1, w2, b2, g_final, layer, bsz, final_norm, tm=1024, tf=1024):
    m, d = xf.shape
    tpb = (m // bsz) // tm
    lay3 = lambda i, f: (layer, 0, 0)
    return pl.pallas_call(
        functools.partial(_mlp_body, final_norm=final_norm),
        grid=(m // tm, w1.shape[2] // tf),
        in_specs=[
            pl.BlockSpec((tm, d), lambda i, f: (i, 0)),
            pl.BlockSpec((None, 1, d), lay3),
            _mod_spec(d, layer, bsz, 3, tpb),
            _mod_spec(d, layer, bsz, 4, tpb),
            _mod_spec(d, layer, bsz, 5, tpb),
            pl.BlockSpec((None, d, tf), lambda i, f: (layer, 0, f)),
            pl.BlockSpec((None, 1, tf), lambda i, f: (layer, 0, f)),
            pl.BlockSpec((None, tf, d), lambda i, f: (layer, f, 0)),
            pl.BlockSpec((None, 1, d), lay3),
            pl.BlockSpec((1, d), lambda i, f: (0, 0)),
        ],
        out_specs=pl.BlockSpec((tm, d), lambda i, f: (i, 0)),
        out_shape=jax.ShapeDtypeStruct((m, d), F32),
        scratch_shapes=[pltpu.VMEM((tm, d), BF16), pltpu.VMEM((tm, LANES), F32)],
        compiler_params=_params("arbitrary", "arbitrary", vmem=60 * MIB),
        name="mlp",
    )(xf, g, modr, modr, modr, w1, b1, w2, b2, g_final)


def kernel(x, c, w_ada, b_ada, g_mix, w_in, b_in, g_v, b_v, w_s, b_s, w_oa, w_ob, w_out,
           g_mlp, w1, b1, w2, b2, g_final):
    bsz, seq, d = x.shape
    depth = w_ada.shape[0]
    m = bsz * seq
    d_a = w_oa.shape[1]
    d_bo = w_ob.shape[1]
    d_b = d_bo * len(DILATIONS)
    assert seq % ATTN_TILE == 0 and d_bo == HEADS_PER_GROUP * HEAD_DIM

    qkv_col0 = 2 * d_a
    gate_col0 = qkv_col0 + 3 * d_b

    n_heads = d_b // HEAD_DIM
    slopes = 2.0 ** (-ALIBI_MAX_EXP * jnp.arange(1, n_heads + 1, dtype=F32) / n_heads)

    def row(a):
        return a.reshape(depth, 1, a.shape[1])

    w_in_b, w_oa_b, w_ob_b, w_out_b, w1_b, w2_b = (
        a.astype(BF16) for a in (w_in, w_oa, w_ob, w_out, w1, w2))
    b_st = jnp.swapaxes(b_s, 1, 2)

    mod = _ada(c, w_ada, b_ada)
    modr = mod.reshape(depth * bsz * 6, 1, d)
    xf = x.reshape(m, d)
    for l in range(depth):
        proj = _in_proj(xf, row(g_mix), modr, w_in_b, row(b_in), l, bsz)
        yb = _attention(proj, slopes, bsz, qkv_col0, d_b)
        x1 = _out_proj(yb, proj, xf, modr, row(g_v), row(b_v), w_s, b_st, w_oa_b, w_ob_b, w_out_b,
                       l, bsz, gate_col0)
        xf = _mlp(x1, row(g_mlp), modr, w1_b, row(b1), w2_b, row(b2), g_final.reshape(1, d),
                  l, bsz, final_norm=(l == depth - 1))
    return xf.reshape(bsz, seq, d)
```

```python
import functools
import math

import jax
import jax.numpy as jnp
from jax import lax
from jax.experimental import pallas as pl
from jax.experimental.pallas import tpu as pltpu

F32 = jnp.float32
BF16 = jnp.bfloat16

HEAD_DIM = 128
CHUNK = 128
Q_BLOCK = 128
SPAN = 128
DILATIONS = (1, 4, 16)
HEADS_PER_GROUP = 4
ALIBI_MAX_EXP = 8.0
EPS = 1e-6
ATTN_TILE = Q_BLOCK * max(DILATIONS)
NEG_INF = float("-inf")
LOG2E = 1.4426950408889634
FAST_STRIDE = 4
LANES = 128
MIB = 1024 * 1024
VMEM_LIMIT = 56 * MIB


def _params(*sem, vmem=VMEM_LIMIT):
    return pltpu.CompilerParams(dimension_semantics=sem, vmem_limit_bytes=vmem)


def _ada_body(c_ref, w_ref, b_ref, o_ref):
    c = c_ref[...]
    act = (c * jax.nn.sigmoid(c)).astype(BF16)
    o_ref[...] = jnp.dot(act, w_ref[...].astype(BF16), preferred_element_type=F32) + b_ref[...]


def _ada(c, w_ada, b_ada):
    depth, d, n = w_ada.shape
    bsz = c.shape[0]
    tn = 1024
    return pl.pallas_call(
        _ada_body,
        grid=(depth, n // tn),
        in_specs=[
            pl.BlockSpec((bsz, d), lambda l, j: (0, 0)),
            pl.BlockSpec((None, d, tn), lambda l, j: (l, 0, j)),
            pl.BlockSpec((None, 1, tn), lambda l, j: (l, 0, j)),
        ],
        out_specs=pl.BlockSpec((None, bsz, tn), lambda l, j: (l, 0, j)),
        out_shape=jax.ShapeDtypeStruct((depth, bsz, n), F32),
        compiler_params=_params("arbitrary", "arbitrary"),
        name="ada",
    )(c, w_ada, b_ada.reshape(depth, 1, n))


def _norm_modulate(h_ref, x_ref, g_ref, sh_ref, sc_ref, rows=16, unroll=16):
    gs = g_ref[...] * (1.0 + sc_ref[...])
    sh = sh_ref[...]

    def body(i, carry):
        r0 = pl.multiple_of(i * rows, rows)
        xv = x_ref[pl.ds(r0, rows), :]
        ms = jnp.mean(xv * xv, axis=-1, keepdims=True)
        h_ref[pl.ds(r0, rows), :] = (xv * lax.rsqrt(ms + EPS) * gs + sh).astype(h_ref.dtype)
        return carry

    lax.fori_loop(0, x_ref.shape[0] // rows, body, 0, unroll=unroll)


def _mod_spec(d, layer, bsz, comp, tiles_per_batch):
    return pl.BlockSpec(
        (None, 1, d),
        lambda i, *_: ((layer * bsz + i // tiles_per_batch) * 6 + comp, 0, 0))


def _inproj_body(x_ref, g_ref, sh_ref, sc_ref, w_ref, b_ref, o_ref, h_ref):
    @pl.when(pl.program_id(1) == 0)
    def _():
        _norm_modulate(h_ref, x_ref, g_ref, sh_ref, sc_ref)

    acc = jnp.dot(h_ref[...], w_ref[...], preferred_element_type=F32)
    o_ref[...] = (acc + b_ref[...]).astype(o_ref.dtype)


def _in_proj(xf, g, modr, w, b, layer, bsz, tm=1024, tn=1792):
    m, d = xf.shape
    n = w.shape[2]
    tpb = (m // bsz) // tm
    return pl.pallas_call(
        _inproj_body,
        grid=(m // tm, n // tn),
        in_specs=[
            pl.BlockSpec((tm, d), lambda i, j: (i, 0)),
            pl.BlockSpec((None, 1, d), lambda i, j: (layer, 0, 0)),
            _mod_spec(d, layer, bsz, 0, tpb),
            _mod_spec(d, layer, bsz, 1, tpb),
            pl.BlockSpec((None, d, tn), lambda i, j: (layer, 0, j)),
            pl.BlockSpec((None, 1, tn), lambda i, j: (layer, 0, j)),
        ],
        out_specs=pl.BlockSpec((tm, tn), lambda i, j: (i, j)),
        out_shape=jax.ShapeDtypeStruct((m, n), BF16),
        scratch_shapes=[pltpu.VMEM((tm, d), BF16)],
        compiler_params=_params("arbitrary", "arbitrary"),
        name="in_proj",
    )(xf, g, modr, modr, w, b)


def _spatial_gating(u_ref, v_ref, gv_ref, bv_ref, ws_ref, bst_ref, o_ref):
    groups = ws_ref.shape[0]
    n_chunks = u_ref.shape[0] // CHUNK
    rowi = lax.broadcasted_iota(jnp.int32, (CHUNK, CHUNK), 0)
    coli = lax.broadcasted_iota(jnp.int32, (CHUNK, CHUNK), 1)
    causal = rowi >= coli
    gv = gv_ref[...]
    bv = bv_ref[...]
    ws = [jnp.where(causal, ws_ref[g], 0.0).astype(BF16) for g in range(groups)]
    for c in range(n_chunks):
        rs = slice(c * CHUNK, (c + 1) * CHUNK)
        v = jax.nn.gelu(v_ref[rs, :].astype(F32))
        mu = jnp.mean(v, axis=-1, keepdims=True)
        vc = v - mu
        var = jnp.mean(vc * vc, axis=-1, keepdims=True)
        vn = (vc * lax.rsqrt(var + EPS) * gv + bv).astype(BF16)
        for g in range(groups):
            cs = slice(g * HEAD_DIM, (g + 1) * HEAD_DIM)
            fv = jnp.dot(ws[g], vn[:, cs], preferred_element_type=F32) + bst_ref[:, g:g + 1]
            u = jax.nn.gelu(u_ref[rs, cs].astype(F32))
            o_ref[rs, cs] = (u * fv).astype(o_ref.dtype)


def _attn_body(slopes_ref, *refs):
    ng = len(DILATIONS)
    q_refs = refs[0:ng]
    kc_refs = refs[ng:2 * ng]
    kp_refs = refs[2 * ng:3 * ng]
    vc_refs = refs[3 * ng:4 * ng]
    vp_refs = refs[4 * ng:5 * ng]
    o_ref = refs[5 * ng]
    qbuf, kbuf, vbuf, qbuf2, kbuf2, vbuf2 = refs[5 * ng + 1:5 * ng + 7]
    u_scr = refs[5 * ng + 7:5 * ng + 7 + ng]
    m_scr = refs[5 * ng + 7 + ng:5 * ng + 7 + 2 * ng]
    l_scr = refs[5 * ng + 7 + 2 * ng:5 * ng + 7 + 3 * ng]

    tile = pl.program_id(1)
    hg = pl.program_id(2)
    t = ATTN_TILE
    q_scale = HEAD_DIM ** -0.5 * LOG2E

    qi = lax.broadcasted_iota(jnp.int32, (Q_BLOCK, 2 * Q_BLOCK), 0) + Q_BLOCK
    ki = lax.broadcasted_iota(jnp.int32, (Q_BLOCK, 2 * Q_BLOCK), 1)
    dist = qi - ki
    in_band = (dist >= 0) & (dist <= SPAN)
    before_start = ki < Q_BLOCK

    def attend(qb, kb, vb, bias):
        s = lax.dot_general(qb, kb, (((1,), (1,)), ((), ())),
                            preferred_element_type=F32) + bias
        mx = jnp.max(s, axis=-1, keepdims=True)
        p = jnp.exp2(s - mx)
        den = jnp.sum(p, axis=-1, keepdims=True)
        u = jnp.dot(p.astype(BF16), vb, preferred_element_type=F32)
        return u, mx, den

    for g, d in enumerate(DILATIONS):
        nblk = t // (Q_BLOCK * d)
        back = Q_BLOCK * d
        q_ref, kc_ref, kp_ref, vc_ref, vp_ref = (q_refs[g], kc_refs[g], kp_refs[g],
                                                 vc_refs[g], vp_refs[g])
        slope = slopes_ref[g * HEADS_PER_GROUP + hg] * LOG2E
        bias = jnp.where(in_band, -slope * (d * dist).astype(F32), NEG_INF)
        bias_first = jnp.where(jnp.logical_and(tile == 0, before_start), NEG_INF, bias)

        if d > 1:
            rows_kv = back + t
            qbuf[...] = q_ref[...].astype(F32) * q_scale
            kbuf[0:back, :] = kp_ref[t - back:t, :].astype(F32)
            kbuf[back:rows_kv, :] = kc_ref[...].astype(F32)
            vbuf[0:back, :] = vp_ref[t - back:t, :].astype(F32)
            vbuf[back:rows_kv, :] = vc_ref[...].astype(F32)
        if d > FAST_STRIDE:
            for j in range(FAST_STRIDE):
                nq, nkv = t // FAST_STRIDE, rows_kv // FAST_STRIDE
                qbuf2[j * nq:(j + 1) * nq, :] = qbuf[pl.ds(j, nq, stride=FAST_STRIDE), :]
                kbuf2[j * nkv:(j + 1) * nkv, :] = kbuf[pl.ds(j, nkv, stride=FAST_STRIDE), :]
                vbuf2[j * nkv:(j + 1) * nkv, :] = vbuf[pl.ds(j, nkv, stride=FAST_STRIDE), :]

        for r in range(d):
            for n in range(nblk):
                if d == 1:
                    lo = n * Q_BLOCK
                    out_rows = pl.ds(lo, Q_BLOCK)
                    qb = (q_ref[out_rows, :].astype(F32) * q_scale).astype(BF16)
                    if n == 0:
                        kb = jnp.concatenate([kp_ref[t - Q_BLOCK:t, :], kc_ref[0:Q_BLOCK, :]], 0)
                        vb = jnp.concatenate([vp_ref[t - Q_BLOCK:t, :], vc_ref[0:Q_BLOCK, :]], 0)
                    else:
                        kb = kc_ref[lo - Q_BLOCK:lo + Q_BLOCK, :]
                        vb = vc_ref[lo - Q_BLOCK:lo + Q_BLOCK, :]
                else:
                    out_rows = pl.ds(r + d * Q_BLOCK * n, Q_BLOCK, stride=d)
                    if d > FAST_STRIDE:
                        j, i, d2 = r % FAST_STRIDE, r // FAST_STRIDE, d // FAST_STRIDE
                        q_lo = j * (t // FAST_STRIDE) + i + d2 * Q_BLOCK * n
                        kv_lo = j * (rows_kv // FAST_STRIDE) + i + d2 * Q_BLOCK * n
                        src = (qbuf2, kbuf2, vbuf2)
                    else:
                        d2 = d
                        q_lo = kv_lo = r + d * Q_BLOCK * n
                        src = (qbuf, kbuf, vbuf)
                    qb = src[0][pl.ds(q_lo, Q_BLOCK, stride=d2), :].astype(BF16)
                    kb = src[1][pl.ds(kv_lo, 2 * Q_BLOCK, stride=d2), :].astype(BF16)
                    vb = src[2][pl.ds(kv_lo, 2 * Q_BLOCK, stride=d2), :].astype(BF16)
                u, mx, den = attend(qb, kb, vb, bias_first if n == 0 else bias)
                u_scr[g][out_rows, :] = u
                m_scr[g][out_rows, :] = jnp.broadcast_to(mx, (Q_BLOCK, HEAD_DIM))
                l_scr[g][out_rows, :] = jnp.broadcast_to(den, (Q_BLOCK, HEAD_DIM))

    def combine(i, carry):
        rows = pl.ds(pl.multiple_of(i * Q_BLOCK, Q_BLOCK), Q_BLOCK)
        ms = [m_scr[g][rows, :] for g in range(ng)]
        top = functools.reduce(jnp.maximum, ms)
        num = jnp.zeros((Q_BLOCK, HEAD_DIM), F32)
        den = jnp.zeros((Q_BLOCK, HEAD_DIM), F32)
        for g in range(ng):
            w = jnp.exp2(ms[g] - top)
            num = num + w * u_scr[g][rows, :]
            den = den + w * l_scr[g][rows, :]
        o_ref[rows, :] = (num / den).astype(o_ref.dtype)
        return carry

    lax.fori_loop(0, t // Q_BLOCK, combine, 0, unroll=2)


def _attention(proj, slopes, bsz, col0, d_b):
    m = proj.shape[0]
    t = ATTN_TILE
    tiles = (m // bsz) // t
    ng = len(DILATIONS)
    hb = col0 // HEAD_DIM
    heads = d_b // HEAD_DIM

    def cur(base, g):
        return pl.BlockSpec(
            (t, HEAD_DIM),
            lambda b, i, h, s: (b * tiles + i, base + g * HEADS_PER_GROUP + h))

    def prev(base, g):
        return pl.BlockSpec(
            (t, HEAD_DIM),
            lambda b, i, h, s: (b * tiles + jnp.maximum(i - 1, 0),
                                base + g * HEADS_PER_GROUP + h))

    in_specs = ([cur(hb, g) for g in range(ng)]
                + [cur(hb + heads, g) for g in range(ng)]
                + [prev(hb + heads, g) for g in range(ng)]
                + [cur(hb + 2 * heads, g) for g in range(ng)]
                + [prev(hb + 2 * heads, g) for g in range(ng)])
    kv_rows = t + Q_BLOCK * max(DILATIONS)
    staging = [pltpu.VMEM((t, HEAD_DIM), F32),
               pltpu.VMEM((kv_rows, HEAD_DIM), F32),
               pltpu.VMEM((kv_rows, HEAD_DIM), F32)]
    scratch = staging + staging + [pltpu.VMEM((t, HEAD_DIM), F32) for _ in range(3 * ng)]
    grid_spec = pltpu.PrefetchScalarGridSpec(
        num_scalar_prefetch=1,
        grid=(bsz, tiles, HEADS_PER_GROUP),
        in_specs=in_specs,
        out_specs=pl.BlockSpec((t, HEAD_DIM), lambda b, i, h, s: (b * tiles + i, h)),
        scratch_shapes=scratch,
    )
    return pl.pallas_call(
        _attn_body,
        grid_spec=grid_spec,
        out_shape=jax.ShapeDtypeStruct((m, HEADS_PER_GROUP * HEAD_DIM), BF16),
        compiler_params=_params("arbitrary", "arbitrary", "arbitrary"),
        name="attention",
    )(slopes, *([proj] * (5 * ng)))


def _outproj_body(u_ref, v_ref, gv_ref, bv_ref, ws_ref, bst_ref, yb_ref, *refs, gate_pieces):
    ga_refs = refs[:gate_pieces]
    gb_refs = refs[gate_pieces:2 * gate_pieces]
    x_ref, gt_ref, woa_ref, wob_ref, wout_ref, o_ref, ya_ref = refs[2 * gate_pieces:]
    _spatial_gating(u_ref, v_ref, gv_ref, bv_ref, ws_ref, bst_ref, ya_ref)
    a = jnp.dot(ya_ref[...], woa_ref[...], preferred_element_type=F32)
    b = jnp.dot(yb_ref[...], wob_ref[...], preferred_element_type=F32)
    gw = ga_refs[0].shape[1]
    merged = jnp.concatenate(
        [(jax.nn.sigmoid(ga_refs[k][...].astype(F32)) * a[:, k * gw:(k + 1) * gw]
          + jax.nn.sigmoid(gb_refs[k][...].astype(F32)) * b[:, k * gw:(k + 1) * gw]).astype(BF16)
         for k in range(gate_pieces)], axis=1)
    o = jnp.dot(merged, wout_ref[...], preferred_element_type=F32)
    o_ref[...] = x_ref[...] + gt_ref[...] * o


def _out_proj(yb, proj, xf, modr, g_v, b_v, w_s, b_st, w_oa, w_ob, w_out, layer, bsz, gate_col0,
              tm=512):
    m, d = xf.shape
    d_a = w_oa.shape[1]
    d_bo = yb.shape[1]
    groups = w_s.shape[1]
    tpb = (m // bsz) // tm
    gw = math.gcd(gate_col0, d)
    pieces = d // gw
    lay3 = lambda i: (layer, 0, 0)

    def gate(col0):
        return [pl.BlockSpec((tm, gw), lambda i, c=col0 // gw + k: (i, c)) for k in range(pieces)]

    return pl.pallas_call(
        functools.partial(_outproj_body, gate_pieces=pieces),
        grid=(m // tm,),
        in_specs=[
            pl.BlockSpec((tm, d_a), lambda i: (i, 0)),
            pl.BlockSpec((tm, d_a), lambda i: (i, 1)),
            pl.BlockSpec((None, 1, d_a), lay3),
            pl.BlockSpec((None, 1, d_a), lay3),
            pl.BlockSpec((None, groups, CHUNK, CHUNK), lambda i: (layer, 0, 0, 0)),
            pl.BlockSpec((None, CHUNK, groups), lay3),
            pl.BlockSpec((tm, d_bo), lambda i: (i, 0)),
            *gate(gate_col0),
            *gate(gate_col0 + d),
            pl.BlockSpec((tm, d), lambda i: (i, 0)),
            _mod_spec(d, layer, bsz, 2, tpb),
            pl.BlockSpec((None, d_a, d), lay3, pipeline_mode=pl.Buffered(1)),
            pl.BlockSpec((None, d_bo, d), lay3, pipeline_mode=pl.Buffered(1)),
            pl.BlockSpec((None, d, d), lay3, pipeline_mode=pl.Buffered(1)),
        ],
        out_specs=pl.BlockSpec((tm, d), lambda i: (i, 0)),
        out_shape=jax.ShapeDtypeStruct((m, d), F32),
        scratch_shapes=[pltpu.VMEM((tm, d_a), BF16)],
        compiler_params=_params("arbitrary", vmem=60 * MIB),
        name="out_proj",
    )(proj, proj, g_v, b_v, w_s, b_st, yb, *([proj] * (2 * pieces)), xf, modr, w_oa, w_ob, w_out)


def _mlp_body(x_ref, g_ref, sh_ref, sc_ref, gt_ref, w1_ref, b1_ref, w2_ref, b2_ref, gf_ref,
              o_ref, h_ref, inv_ref, *, final_norm, rows=16):
    f = pl.program_id(1)
    n_row_steps = x_ref.shape[0] // rows

    @pl.when(f == 0)
    def _():
        _norm_modulate(h_ref, x_ref, g_ref, sh_ref, sc_ref)
        o_ref[...] = jnp.zeros_like(o_ref)

    hid = jnp.dot(h_ref[...], w1_ref[...], preferred_element_type=F32) + b1_ref[...]
    hid = jnp.square(jnp.maximum(hid, 0.0)).astype(BF16)
    o_ref[...] += jnp.dot(hid, w2_ref[...], preferred_element_type=F32)

    @pl.when(f == pl.num_programs(1) - 1)
    def _():
        gt = gt_ref[...]
        b2 = b2_ref[...]

        def residual(i, carry):
            r = pl.ds(pl.multiple_of(i * rows, rows), rows)
            y = x_ref[r, :] + gt * (o_ref[r, :] + b2)
            o_ref[r, :] = y
            if final_norm:
                ms = jnp.mean(y * y, axis=-1, keepdims=True)
                inv_ref[r, :] = jnp.broadcast_to(lax.rsqrt(ms + EPS), (rows, inv_ref.shape[1]))
            return carry

        lax.fori_loop(0, n_row_steps, residual, 0, unroll=8)

        if final_norm:
            gf = gf_ref[...]
            reps = o_ref.shape[1] // inv_ref.shape[1]

            def normalise(i, carry):
                r = pl.ds(pl.multiple_of(i * rows, rows), rows)
                inv = jnp.concatenate([inv_ref[r, :]] * reps, axis=1)
                o_ref[r, :] = o_ref[r, :] * inv * gf
                return carry

            lax.fori_loop(0, n_row_steps, normalise, 0, unroll=8)


def _mlp(xf, g, modr, w1, b1, w2, b2, g_final, layer, bsz, final_norm, tm=1024, tf=1024):
    m, d = xf.shape
    tpb = (m // bsz) // tm
    lay3 = lambda i, f: (layer, 0, 0)
    return pl.pallas_call(
        functools.partial(_mlp_body, final_norm=final_norm),
        grid=(m // tm, w1.shape[2] // tf),
        in_specs=[
            pl.BlockSpec((tm, d), lambda i, f: (i, 0)),
            pl.BlockSpec((None, 1, d), lay3),
            _mod_spec(d, layer, bsz, 3, tpb),
            _mod_spec(d, layer, bsz, 4, tpb),
            _mod_spec(d, layer, bsz, 5, tpb),
            pl.BlockSpec((None, d, tf), lambda i, f: (layer, 0, f)),
            pl.BlockSpec((None, 1, tf), lambda i, f: (layer, 0, f)),
            pl.BlockSpec((None, tf, d), lambda i, f: (layer, f, 0)),
            pl.BlockSpec((None, 1, d), lay3),
            pl.BlockSpec((1, d), lambda i, f: (0, 0)),
        ],
        out_specs=pl.BlockSpec((tm, d), lambda i, f: (i, 0)),
        out_shape=jax.ShapeDtypeStruct((m, d), F32),
        scratch_shapes=[pltpu.VMEM((tm, d), BF16), pltpu.VMEM((tm, LANES), F32)],
        compiler_params=_params("arbitrary", "arbitrary", vmem=60 * MIB),
        name="mlp",
    )(xf, g, modr, modr, modr, w1, b1, w2, b2, g_final)


def kernel(x, c, w_ada, b_ada, g_mix, w_in, b_in, g_v, b_v, w_s, b_s, w_oa, w_ob, w_out,
           g_mlp, w1, b1, w2, b2, g_final):
    bsz, seq, d = x.shape
    depth = w_ada.shape[0]
    m = bsz * seq
    d_a = w_oa.shape[1]
    d_bo = w_ob.shape[1]
    d_b = d_bo * len(DILATIONS)
    assert seq % ATTN_TILE == 0 and d_bo == HEADS_PER_GROUP * HEAD_DIM

    qkv_col0 = 2 * d_a
    gate_col0 = qkv_col0 + 3 * d_b

    n_heads = d_b // HEAD_DIM
    slopes = 2.0 ** (-ALIBI_MAX_EXP * jnp.arange(1, n_heads + 1, dtype=F32) / n_heads)

    def row(a):
        return a.reshape(depth, 1, a.shape[1])

    w_in_b, w_oa_b, w_ob_b, w_out_b, w1_b, w2_b = (
        a.astype(BF16) for a in (w_in, w_oa, w_ob, w_out, w1, w2))
    b_st = jnp.swapaxes(b_s, 1, 2)

    mod = _ada(c, w_ada, b_ada)
    modr = mod.reshape(depth * bsz * 6, 1, d)
    xf = x.reshape(m, d)
    for l in range(depth):
        proj = _in_proj(xf, row(g_mix), modr, w_in_b, row(b_in), l, bsz)
        yb = _attention(proj, slopes, bsz, qkv_col0, d_b)
        x1 = _out_proj(yb, proj, xf, modr, row(g_v), row(b_v), w_s, b_st, w_oa_b, w_ob_b, w_out_b,
                       l, bsz, gate_col0)
        xf = _mlp(x1, row(g_mlp), modr, w1_b, row(b1), w2_b, row(b2), g_final.reshape(1, d),
                  l, bsz, final_norm=(l == depth - 1))
    return xf.reshape(bsz, seq, d)
```

```python
import functools
import math

import jax
import jax.numpy as jnp
from jax import lax
from jax.experimental import pallas as pl
from jax.experimental.pallas import tpu as pltpu

F32 = jnp.float32
BF16 = jnp.bfloat16

HEAD_DIM = 128
CHUNK = 128
Q_BLOCK = 128
SPAN = 128
DILATIONS = (1, 4, 16)
HEADS_PER_GROUP = 4
ALIBI_MAX_EXP = 8.0
EPS = 1e-6
ATTN_TILE = Q_BLOCK * max(DILATIONS)
NEG_INF = float("-inf")
LOG2E = 1.4426950408889634
FAST_STRIDE = 4
LANES = 128
MIB = 1024 * 1024
VMEM_LIMIT = 56 * MIB


def _params(*sem, vmem=VMEM_LIMIT):
    return pltpu.CompilerParams(dimension_semantics=sem, vmem_limit_bytes=vmem)


def _ada_body(c_ref, w_ref, b_ref, o_ref):
    c = c_ref[...]
    act = (c * jax.nn.sigmoid(c)).astype(BF16)
    o_ref[...] = jnp.dot(act, w_ref[...].astype(BF16), preferred_element_type=F32) + b_ref[...]


def _ada(c, w_ada, b_ada):
    depth, d, n = w_ada.shape
    bsz = c.shape[0]
    tn = 1024
    return pl.pallas_call(
        _ada_body,
        grid=(depth, n // tn),
        in_specs=[
            pl.BlockSpec((bsz, d), lambda l, j: (0, 0)),
            pl.BlockSpec((None, d, tn), lambda l, j: (l, 0, j)),
            pl.BlockSpec((None, 1, tn), lambda l, j: (l, 0, j)),
        ],
        out_specs=pl.BlockSpec((None, bsz, tn), lambda l, j: (l, 0, j)),
        out_shape=jax.ShapeDtypeStruct((depth, bsz, n), F32),
        compiler_params=_params("arbitrary", "arbitrary"),
        name="ada",
    )(c, w_ada, b_ada.reshape(depth, 1, n))


def _norm_rows(h_ref, x_ref, gs, sh, r):
    xv = x_ref[r, :]
    ms = jnp.mean(xv * xv, axis=-1, keepdims=True)
    h_ref[r, :] = (xv * lax.rsqrt(ms + EPS) * gs + sh).astype(h_ref.dtype)


def _norm_modulate(h_ref, x_ref, g_ref, sh_ref, sc_ref, rows=16, unroll=16):
    gs = g_ref[...] * (1.0 + sc_ref[...])
    sh = sh_ref[...]

    def body(i, carry):
        _norm_rows(h_ref, x_ref, gs, sh, pl.ds(pl.multiple_of(i * rows, rows), rows))
        return carry

    lax.fori_loop(0, x_ref.shape[0] // rows, body, 0, unroll=unroll)


def _mod_spec(d, layer, bsz, comp, tiles_per_batch):
    return pl.BlockSpec(
        (None, 1, d),
        lambda i, *_: ((layer * bsz + i // tiles_per_batch) * 6 + comp, 0, 0))


def _inproj_body(x_ref, g_ref, sh_ref, sc_ref, w_ref, b_ref, o_ref, h0_ref, h1_ref, *,
                 rows=16):
    i = pl.program_id(0)
    j = pl.program_id(1)
    tm = x_ref.shape[0]
    per_step = -(-tm // (pl.num_programs(1) * rows)) * rows

    def normalise(h_ref):
        gs = g_ref[...] * (1.0 + sc_ref[...])
        sh = sh_ref[...]
        start = pl.multiple_of(jnp.minimum(j * per_step, tm - per_step), rows)
        for k in range(per_step // rows):
            _norm_rows(h_ref, x_ref, gs, sh, pl.ds(start + k * rows, rows))

    def project(h_ref):
        acc = jnp.dot(h_ref[...], w_ref[...], preferred_element_type=F32)
        o_ref[...] = (acc + b_ref[...]).astype(o_ref.dtype)

    @pl.when(i == 0)
    def _():
        normalise(h0_ref)

    @pl.when(jnp.logical_and(i > 0, i % 2 == 1))
    def _():
        project(h0_ref)
        normalise(h1_ref)

    @pl.when(jnp.logical_and(i > 0, i % 2 == 0))
    def _():
        project(h1_ref)
        normalise(h0_ref)


def _in_proj(xf, g, modr, w, b, layer, bsz, tm=1024, tn=1792):
    m, d = xf.shape
    n = w.shape[2]
    n_tiles = m // tm
    tpb = (m // bsz) // tm
    ahead = lambda i: jnp.minimum(i, n_tiles - 1)
    tile = lambda i: jnp.maximum(i - 1, 0)
    col = lambda i, j: jnp.where(i == 0, 0, j)

    def mod(comp):
        return pl.BlockSpec(
            (None, 1, d), lambda i, j: ((layer * bsz + ahead(i) // tpb) * 6 + comp, 0, 0))

    return pl.pallas_call(
        _inproj_body,
        grid=(n_tiles + 1, n // tn),
        in_specs=[
            pl.BlockSpec((tm, d), lambda i, j: (ahead(i), 0)),
            pl.BlockSpec((None, 1, d), lambda i, j: (layer, 0, 0)),
            mod(0),
            mod(1),
            pl.BlockSpec((None, d, tn), lambda i, j: (layer, 0, col(i, j))),
            pl.BlockSpec((None, 1, tn), lambda i, j: (layer, 0, col(i, j))),
        ],
        out_specs=pl.BlockSpec((tm, tn), lambda i, j: (tile(i), col(i, j))),
        out_shape=jax.ShapeDtypeStruct((m, n), BF16),
        scratch_shapes=[pltpu.VMEM((tm, d), BF16), pltpu.VMEM((tm, d), BF16)],
        compiler_params=_params("arbitrary", "arbitrary"),
        name="in_proj",
    )(xf, g, modr, modr, w, b)


def _spatial_gating(u_ref, v_ref, gv_ref, bv_ref, ws_ref, bst_ref, o_ref):
    groups = ws_ref.shape[0]
    n_chunks = u_ref.shape[0] // CHUNK
    rowi = lax.broadcasted_iota(jnp.int32, (CHUNK, CHUNK), 0)
    coli = lax.broadcasted_iota(jnp.int32, (CHUNK, CHUNK), 1)
    causal = rowi >= coli
    gv = gv_ref[...]
    bv = bv_ref[...]
    ws = [jnp.where(causal, ws_ref[g], 0.0).astype(BF16) for g in range(groups)]
    for c in range(n_chunks):
        rs = slice(c * CHUNK, (c + 1) * CHUNK)
        v = jax.nn.gelu(v_ref[rs, :].astype(F32))
        mu = jnp.mean(v, axis=-1, keepdims=True)
        vc = v - mu
        var = jnp.mean(vc * vc, axis=-1, keepdims=True)
        vn = (vc * lax.rsqrt(var + EPS) * gv + bv).astype(BF16)
        for g in range(groups):
            cs = slice(g * HEAD_DIM, (g + 1) * HEAD_DIM)
            fv = jnp.dot(ws[g], vn[:, cs], preferred_element_type=F32) + bst_ref[:, g:g + 1]
            u = jax.nn.gelu(u_ref[rs, cs].astype(F32))
            o_ref[rs, cs] = (u * fv).astype(o_ref.dtype)


def _attn_body(slopes_ref, *refs):
    ng = len(DILATIONS)
    q_refs = refs[0:ng]
    kc_refs = refs[ng:2 * ng]
    kp_refs = refs[2 * ng:3 * ng]
    vc_refs = refs[3 * ng:4 * ng]
    vp_refs = refs[4 * ng:5 * ng]
    o_ref = refs[5 * ng]
    qbuf, kbuf, vbuf, qbuf2, kbuf2, vbuf2 = refs[5 * ng + 1:5 * ng + 7]
    u_scr = refs[5 * ng + 7:5 * ng + 7 + ng]
    m_scr = refs[5 * ng + 7 + ng:5 * ng + 7 + 2 * ng]
    l_scr = refs[5 * ng + 7 + 2 * ng:5 * ng + 7 + 3 * ng]

    tile = pl.program_id(1)
    hg = pl.program_id(2)
    t = ATTN_TILE
    q_scale = HEAD_DIM ** -0.5 * LOG2E

    qi = lax.broadcasted_iota(jnp.int32, (Q_BLOCK, 2 * Q_BLOCK), 0) + Q_BLOCK
    ki = lax.broadcasted_iota(jnp.int32, (Q_BLOCK, 2 * Q_BLOCK), 1)
    dist = qi - ki
    in_band = (dist >= 0) & (dist <= SPAN)
    before_start = ki < Q_BLOCK

    def attend(qb, kb, vb, bias):
        s = lax.dot_general(qb, kb, (((1,), (1,)), ((), ())),
                            preferred_element_type=F32) + bias
        mx = jnp.max(s, axis=-1, keepdims=True)
        p = jnp.exp2(s - mx)
        den = jnp.sum(p, axis=-1, keepdims=True)
        u = jnp.dot(p.astype(BF16), vb, preferred_element_type=F32)
        return u, mx, den

    for g, d in enumerate(DILATIONS):
        nblk = t // (Q_BLOCK * d)
        back = Q_BLOCK * d
        q_ref, kc_ref, kp_ref, vc_ref, vp_ref = (q_refs[g], kc_refs[g], kp_refs[g],
                                                 vc_refs[g], vp_refs[g])
        slope = slopes_ref[g * HEADS_PER_GROUP + hg] * LOG2E
        bias = jnp.where(in_band, -slope * (d * dist).astype(F32), NEG_INF)
        bias_first = jnp.where(jnp.logical_and(tile == 0, before_start), NEG_INF, bias)

        if d > 1:
            rows_kv = back + t
            qbuf[...] = q_ref[...].astype(F32) * q_scale
            kbuf[0:back, :] = kp_ref[t - back:t, :].astype(F32)
            kbuf[back:rows_kv, :] = kc_ref[...].astype(F32)
            vbuf[0:back, :] = vp_ref[t - back:t, :].astype(F32)
            vbuf[back:rows_kv, :] = vc_ref[...].astype(F32)
        if d > FAST_STRIDE:
            for j in range(FAST_STRIDE):
                nq, nkv = t // FAST_STRIDE, rows_kv // FAST_STRIDE
                qbuf2[j * nq:(j + 1) * nq, :] = qbuf[pl.ds(j, nq, stride=FAST_STRIDE), :]
                kbuf2[j * nkv:(j + 1) * nkv, :] = kbuf[pl.ds(j, nkv, stride=FAST_STRIDE), :]
                vbuf2[j * nkv:(j + 1) * nkv, :] = vbuf[pl.ds(j, nkv, stride=FAST_STRIDE), :]

        for r in range(d):
            for n in range(nblk):
                if d == 1:
                    lo = n * Q_BLOCK
                    out_rows = pl.ds(lo, Q_BLOCK)
                    qb = (q_ref[out_rows, :].astype(F32) * q_scale).astype(BF16)
                    if n == 0:
                        kb = jnp.concatenate([kp_ref[t - Q_BLOCK:t, :], kc_ref[0:Q_BLOCK, :]], 0)
                        vb = jnp.concatenate([vp_ref[t - Q_BLOCK:t, :], vc_ref[0:Q_BLOCK, :]], 0)
                    else:
                        kb = kc_ref[lo - Q_BLOCK:lo + Q_BLOCK, :]
                        vb = vc_ref[lo - Q_BLOCK:lo + Q_BLOCK, :]
                else:
                    out_rows = pl.ds(r + d * Q_BLOCK * n, Q_BLOCK, stride=d)
                    if d > FAST_STRIDE:
                        j, i, d2 = r % FAST_STRIDE, r // FAST_STRIDE, d // FAST_STRIDE
                        q_lo = j * (t // FAST_STRIDE) + i + d2 * Q_BLOCK * n
                        kv_lo = j * (rows_kv // FAST_STRIDE) + i + d2 * Q_BLOCK * n
                        src = (qbuf2, kbuf2, vbuf2)
                    else:
                        d2 = d
                        q_lo = kv_lo = r + d * Q_BLOCK * n
                        src = (qbuf, kbuf, vbuf)
                    qb = src[0][pl.ds(q_lo, Q_BLOCK, stride=d2), :].astype(BF16)
                    kb = src[1][pl.ds(kv_lo, 2 * Q_BLOCK, stride=d2), :].astype(BF16)
                    vb = src[2][pl.ds(kv_lo, 2 * Q_BLOCK, stride=d2), :].astype(BF16)
                u, mx, den = attend(qb, kb, vb, bias_first if n == 0 else bias)
                u_scr[g][out_rows, :] = u
                m_scr[g][out_rows, :] = jnp.broadcast_to(mx, (Q_BLOCK, HEAD_DIM))
                l_scr[g][out_rows, :] = jnp.broadcast_to(den, (Q_BLOCK, HEAD_DIM))

    def combine(i, carry):
        rows = pl.ds(pl.multiple_of(i * Q_BLOCK, Q_BLOCK), Q_BLOCK)
        ms = [m_scr[g][rows, :] for g in range(ng)]
        top = functools.reduce(jnp.maximum, ms)
        num = jnp.zeros((Q_BLOCK, HEAD_DIM), F32)
        den = jnp.zeros((Q_BLOCK, HEAD_DIM), F32)
        for g in range(ng):
            w = jnp.exp2(ms[g] - top)
            num = num + w * u_scr[g][rows, :]
            den = den + w * l_scr[g][rows, :]
        o_ref[rows, :] = (num / den).astype(o_ref.dtype)
        return carry

    lax.fori_loop(0, t // Q_BLOCK, combine, 0, unroll=2)


def _attention(proj, slopes, bsz, col0, d_b):
    m = proj.shape[0]
    t = ATTN_TILE
    tiles = (m // bsz) // t
    ng = len(DILATIONS)
    hb = col0 // HEAD_DIM
    heads = d_b // HEAD_DIM

    def cur(base, g):
        return pl.BlockSpec(
            (t, HEAD_DIM),
            lambda b, i, h, s: (b * tiles + i, base + g * HEADS_PER_GROUP + h))

    def prev(base, g):
        return pl.BlockSpec(
            (t, HEAD_DIM),
            lambda b, i, h, s: (b * tiles + jnp.maximum(i - 1, 0),
                                base + g * HEADS_PER_GROUP + h))

    in_specs = ([cur(hb, g) for g in range(ng)]
                + [cur(hb + heads, g) for g in range(ng)]
                + [prev(hb + heads, g) for g in range(ng)]
                + [cur(hb + 2 * heads, g) for g in range(ng)]
                + [prev(hb + 2 * heads, g) for g in range(ng)])
    kv_rows = t + Q_BLOCK * max(DILATIONS)
    staging = [pltpu.VMEM((t, HEAD_DIM), F32),
               pltpu.VMEM((kv_rows, HEAD_DIM), F32),
               pltpu.VMEM((kv_rows, HEAD_DIM), F32)]
    scratch = staging + staging + [pltpu.VMEM((t, HEAD_DIM), F32) for _ in range(3 * ng)]
    grid_spec = pltpu.PrefetchScalarGridSpec(
        num_scalar_prefetch=1,
        grid=(bsz, tiles, HEADS_PER_GROUP),
        in_specs=in_specs,
        out_specs=pl.BlockSpec((t, HEAD_DIM), lambda b, i, h, s: (b * tiles + i, h)),
        scratch_shapes=scratch,
    )
    return pl.pallas_call(
        _attn_body,
        grid_spec=grid_spec,
        out_shape=jax.ShapeDtypeStruct((m, HEADS_PER_GROUP * HEAD_DIM), BF16),
        compiler_params=_params("arbitrary", "arbitrary", "arbitrary"),
        name="attention",
    )(slopes, *([proj] * (5 * ng)))


def _outproj_body(u_ref, v_ref, gv_ref, bv_ref, ws_ref, bst_ref, yb_ref, *refs, gate_pieces):
    ga_refs = refs[:gate_pieces]
    gb_refs = refs[gate_pieces:2 * gate_pieces]
    x_ref, gt_ref, woa_ref, wob_ref, wout_ref, o_ref, ya0_ref, ya1_ref = refs[2 * gate_pieces:]
    s = pl.program_id(0)

    def gating(dst_ref):
        _spatial_gating(u_ref, v_ref, gv_ref, bv_ref, ws_ref, bst_ref, dst_ref)

    def project(ya_ref):
        a = jnp.dot(ya_ref[...], woa_ref[...], preferred_element_type=F32)
        b = jnp.dot(yb_ref[...], wob_ref[...], preferred_element_type=F32)
        gw = ga_refs[0].shape[1]
        merged = jnp.concatenate(
            [(jax.nn.sigmoid(ga_refs[k][...].astype(F32)) * a[:, k * gw:(k + 1) * gw]
              + jax.nn.sigmoid(gb_refs[k][...].astype(F32)) * b[:, k * gw:(k + 1) * gw]
              ).astype(BF16) for k in range(gate_pieces)], axis=1)
        o = jnp.dot(merged, wout_ref[...], preferred_element_type=F32)
        o_ref[...] = x_ref[...] + gt_ref[...] * o

    @pl.when(s == 0)
    def _():
        gating(ya0_ref)

    @pl.when(jnp.logical_and(s > 0, s % 2 == 1))
    def _():
        project(ya0_ref)
        gating(ya1_ref)

    @pl.when(jnp.logical_and(s > 0, s % 2 == 0))
    def _():
        project(ya1_ref)
        gating(ya0_ref)


def _out_proj(yb, proj, xf, modr, g_v, b_v, w_s, b_st, w_oa, w_ob, w_out, layer, bsz, gate_col0,
              tm=512):
    m, d = xf.shape
    d_a = w_oa.shape[1]
    d_bo = yb.shape[1]
    groups = w_s.shape[1]
    tpb = (m // bsz) // tm
    gw = math.gcd(gate_col0, d)
    pieces = d // gw
    lay3 = lambda i: (layer, 0, 0)

    n_tiles = m // tm
    ahead = lambda s: jnp.minimum(s, n_tiles - 1)
    tile = lambda s: jnp.maximum(s - 1, 0)

    def gate(col0):
        return [pl.BlockSpec((tm, gw), lambda s, c=col0 // gw + k: (tile(s), c))
                for k in range(pieces)]

    return pl.pallas_call(
        functools.partial(_outproj_body, gate_pieces=pieces),
        grid=(n_tiles + 1,),
        in_specs=[
            pl.BlockSpec((tm, d_a), lambda s: (ahead(s), 0)),
            pl.BlockSpec((tm, d_a), lambda s: (ahead(s), 1)),
            pl.BlockSpec((None, 1, d_a), lay3),
            pl.BlockSpec((None, 1, d_a), lay3),
            pl.BlockSpec((None, groups, CHUNK, CHUNK), lambda s: (layer, 0, 0, 0)),
            pl.BlockSpec((None, CHUNK, groups), lay3),
            pl.BlockSpec((tm, d_bo), lambda s: (tile(s), 0)),
            *gate(gate_col0),
            *gate(gate_col0 + d),
            pl.BlockSpec((tm, d), lambda s: (tile(s), 0)),
            pl.BlockSpec((None, 1, d),
                         lambda s: ((layer * bsz + tile(s) // tpb) * 6 + 2, 0, 0)),
            pl.BlockSpec((None, d_a, d), lay3, pipeline_mode=pl.Buffered(1)),
            pl.BlockSpec((None, d_bo, d), lay3, pipeline_mode=pl.Buffered(1)),
            pl.BlockSpec((None, d, d), lay3, pipeline_mode=pl.Buffered(1)),
        ],
        out_specs=pl.BlockSpec((tm, d), lambda s: (tile(s), 0)),
        out_shape=jax.ShapeDtypeStruct((m, d), F32),
        scratch_shapes=[pltpu.VMEM((tm, d_a), BF16), pltpu.VMEM((tm, d_a), BF16)],
        compiler_params=_params("arbitrary", vmem=60 * MIB),
        name="out_proj",
    )(proj, proj, g_v, b_v, w_s, b_st, yb, *([proj] * (2 * pieces)), xf, modr, w_oa, w_ob, w_out)


def _mlp_body(x_ref, g_ref, sh_ref, sc_ref, gt_ref, w1_ref, b1_ref, w2_ref, b2_ref, gf_ref,
              o_ref, h_ref, inv_ref, *, final_norm, rows=16):
    f = pl.program_id(1)
    n_row_steps = x_ref.shape[0] // rows

    @pl.when(f == 0)
    def _():
        _norm_modulate(h_ref, x_ref, g_ref, sh_ref, sc_ref)
        o_ref[...] = jnp.zeros_like(o_ref)

    hid = jnp.dot(h_ref[...], w1_ref[...], preferred_element_type=F32) + b1_ref[...]
    hid = jnp.square(jnp.maximum(hid, 0.0)).astype(BF16)
    o_ref[...] += jnp.dot(hid, w2_ref[...], preferred_element_type=F32)

    @pl.when(f == pl.num_programs(1) - 1)
    def _():
        gt = gt_ref[...]
        b2 = b2_ref[...]

        def residual(i, carry):
            r = pl.ds(pl.multiple_of(i * rows, rows), rows)
            y = x_ref[r, :] + gt * (o_ref[r, :] + b2)
            o_ref[r, :] = y
            if final_norm:
                ms = jnp.mean(y * y, axis=-1, keepdims=True)
                inv_ref[r, :] = jnp.broadcast_to(lax.rsqrt(ms + EPS), (rows, inv_ref.shape[1]))
            return carry

        lax.fori_loop(0, n_row_steps, residual, 0, unroll=8)

        if final_norm:
            gf = gf_ref[...]
            reps = o_ref.shape[1] // inv_ref.shape[1]

            def normalise(i, carry):
                r = pl.ds(pl.multiple_of(i * rows, rows), rows)
                inv = jnp.concatenate([inv_ref[r, :]] * reps, axis=1)
                o_ref[r, :] = o_ref[r, :] * inv * gf
                return carry

            lax.fori_loop(0, n_row_steps, normalise, 0, unroll=8)


def _mlp(xf, g, modr, w1, b1, w2, b2, g_final, layer, bsz, final_norm, tm=1024, tf=1024):
    m, d = xf.shape
    tpb = (m // bsz) // tm
    lay3 = lambda i, f: (layer, 0, 0)
    return pl.pallas_call(
        functools.partial(_mlp_body, final_norm=final_norm),
        grid=(m // tm, w1.shape[2] // tf),
        in_specs=[
            pl.BlockSpec((tm, d), lambda i, f: (i, 0)),
            pl.BlockSpec((None, 1, d), lay3),
            _mod_spec(d, layer, bsz, 3, tpb),
            _mod_spec(d, layer, bsz, 4, tpb),
            _mod_spec(d, layer, bsz, 5, tpb),
            pl.BlockSpec((None, d, tf), lambda i, f: (layer, 0, f)),
            pl.BlockSpec((None, 1, tf), lambda i, f: (layer, 0, f)),
            pl.BlockSpec((None, tf, d), lambda i, f: (layer, f, 0)),
            pl.BlockSpec((None, 1, d), lay3),
            pl.BlockSpec((1, d), lambda i, f: (0, 0)),
        ],
        out_specs=pl.BlockSpec((tm, d), lambda i, f: (i, 0)),
        out_shape=jax.ShapeDtypeStruct((m, d), F32),
        scratch_shapes=[pltpu.VMEM((tm, d), BF16), pltpu.VMEM((tm, LANES), F32)],
        compiler_params=_params("arbitrary", "arbitrary", vmem=60 * MIB),
        name="mlp",
    )(xf, g, modr, modr, modr, w1, b1, w2, b2, g_final)


def kernel(x, c, w_ada, b_ada, g_mix, w_in, b_in, g_v, b_v, w_s, b_s, w_oa, w_ob, w_out,
           g_mlp, w1, b1, w2, b2, g_final):
    bsz, seq, d = x.shape
    depth = w_ada.shape[0]
    m = bsz * seq
    d_a = w_oa.shape[1]
    d_bo = w_ob.shape[1]
    d_b = d_bo * len(DILATIONS)
    assert seq % ATTN_TILE == 0 and d_bo == HEADS_PER_GROUP * HEAD_DIM

    qkv_col0 = 2 * d_a
    gate_col0 = qkv_col0 + 3 * d_b

    n_heads = d_b // HEAD_DIM
    slopes = 2.0 ** (-ALIBI_MAX_EXP * jnp.arange(1, n_heads + 1, dtype=F32) / n_heads)

    def row(a):
        return a.reshape(depth, 1, a.shape[1])

    w_in_b, w_oa_b, w_ob_b, w_out_b, w1_b, w2_b = (
        a.astype(BF16) for a in (w_in, w_oa, w_ob, w_out, w1, w2))
    b_st = jnp.swapaxes(b_s, 1, 2)

    mod = _ada(c, w_ada, b_ada)
    modr = mod.reshape(depth * bsz * 6, 1, d)
    xf = x.reshape(m, d)
    for l in range(depth):
        proj = _in_proj(xf, row(g_mix), modr, w_in_b, row(b_in), l, bsz)
        yb = _attention(proj, slopes, bsz, qkv_col0, d_b)
        x1 = _out_proj(yb, proj, xf, modr, row(g_v), row(b_v), w_s, b_st, w_oa_b, w_ob_b, w_out_b,
                       l, bsz, gate_col0)
        xf = _mlp(x1, row(g_mlp), modr, w1_b, row(b1), w2_b, row(b2), g_final.reshape(1, d),
                  l, bsz, final_norm=(l == depth - 1))
    return xf.reshape(bsz, seq, d)
```

```python
import functools
import math

import jax
import jax.numpy as jnp
from jax import lax
from jax.experimental import pallas as pl
from jax.experimental.pallas import tpu as pltpu

F32 = jnp.float32
BF16 = jnp.bfloat16

HEAD_DIM = 128
CHUNK = 128
Q_BLOCK = 128
SPAN = 128
DILATIONS = (1, 4, 16)
HEADS_PER_GROUP = 4
ALIBI_MAX_EXP = 8.0
EPS = 1e-6
ATTN_TILE = Q_BLOCK * max(DILATIONS)
NEG_INF = float("-inf")
LOG2E = 1.4426950408889634
FAST_STRIDE = 4
LANES = 128
MIB = 1024 * 1024
VMEM_LIMIT = 56 * MIB


def _params(*sem, vmem=VMEM_LIMIT):
    return pltpu.CompilerParams(dimension_semantics=sem, vmem_limit_bytes=vmem)


def _ada_body(c_ref, w_ref, b_ref, o_ref):
    c = c_ref[...]
    act = (c * jax.nn.sigmoid(c)).astype(BF16)
    o_ref[...] = jnp.dot(act, w_ref[...].astype(BF16), preferred_element_type=F32) + b_ref[...]


def _ada(c, w_ada, b_ada):
    depth, d, n = w_ada.shape
    bsz = c.shape[0]
    tn = 1024
    return pl.pallas_call(
        _ada_body,
        grid=(depth, n // tn),
        in_specs=[
            pl.BlockSpec((bsz, d), lambda l, j: (0, 0)),
            pl.BlockSpec((None, d, tn), lambda l, j: (l, 0, j)),
            pl.BlockSpec((None, 1, tn), lambda l, j: (l, 0, j)),
        ],
        out_specs=pl.BlockSpec((None, bsz, tn), lambda l, j: (l, 0, j)),
        out_shape=jax.ShapeDtypeStruct((depth, bsz, n), F32),
        compiler_params=_params("arbitrary", "arbitrary"),
        name="ada",
    )(c, w_ada, b_ada.reshape(depth, 1, n))


def _norm_rows(h_ref, x_ref, gs, sh, r):
    xv = x_ref[r, :]
    ms = jnp.mean(xv * xv, axis=-1, keepdims=True)
    h_ref[r, :] = (xv * lax.rsqrt(ms + EPS) * gs + sh).astype(h_ref.dtype)


def _norm_modulate(h_ref, x_ref, g_ref, sh_ref, sc_ref, rows=16, unroll=16):
    gs = g_ref[...] * (1.0 + sc_ref[...])
    sh = sh_ref[...]

    def body(i, carry):
        _norm_rows(h_ref, x_ref, gs, sh, pl.ds(pl.multiple_of(i * rows, rows), rows))
        return carry

    lax.fori_loop(0, x_ref.shape[0] // rows, body, 0, unroll=unroll)


def _mod_spec(d, layer, bsz, comp, tiles_per_batch):
    return pl.BlockSpec(
        (None, 1, d),
        lambda i, *_: ((layer * bsz + i // tiles_per_batch) * 6 + comp, 0, 0))


def _inproj_body(x_ref, g_ref, sh_ref, sc_ref, w_ref, b_ref, o_ref, h0_ref, h1_ref, *,
                 rows=16):
    i = pl.program_id(0)
    j = pl.program_id(1)
    tm = x_ref.shape[0]
    per_step = -(-tm // (pl.num_programs(1) * rows)) * rows

    def normalise(h_ref):
        gs = g_ref[...] * (1.0 + sc_ref[...])
        sh = sh_ref[...]
        start = pl.multiple_of(jnp.minimum(j * per_step, tm - per_step), rows)
        for k in range(per_step // rows):
            _norm_rows(h_ref, x_ref, gs, sh, pl.ds(start + k * rows, rows))

    def project(h_ref):
        acc = jnp.dot(h_ref[...], w_ref[...], preferred_element_type=F32)
        o_ref[...] = (acc + b_ref[...]).astype(o_ref.dtype)

    @pl.when(i == 0)
    def _():
        normalise(h0_ref)

    @pl.when(jnp.logical_and(i > 0, i % 2 == 1))
    def _():
        project(h0_ref)
        normalise(h1_ref)

    @pl.when(jnp.logical_and(i > 0, i % 2 == 0))
    def _():
        project(h1_ref)
        normalise(h0_ref)


def _in_proj(xf, g, modr, w, b, layer, bsz, tm=1024, tn=1792):
    m, d = xf.shape
    n = w.shape[2]
    n_tiles = m // tm
    tpb = (m // bsz) // tm
    ahead = lambda i: jnp.minimum(i, n_tiles - 1)
    tile = lambda i: jnp.maximum(i - 1, 0)
    col = lambda i, j: jnp.where(i == 0, 0, j)

    def mod(comp):
        return pl.BlockSpec(
            (None, 1, d), lambda i, j: ((layer * bsz + ahead(i) // tpb) * 6 + comp, 0, 0))

    return pl.pallas_call(
        _inproj_body,
        grid=(n_tiles + 1, n // tn),
        in_specs=[
            pl.BlockSpec((tm, d), lambda i, j: (ahead(i), 0)),
            pl.BlockSpec((None, 1, d), lambda i, j: (layer, 0, 0)),
            mod(0),
            mod(1),
            pl.BlockSpec((None, d, tn), lambda i, j: (layer, 0, col(i, j))),
            pl.BlockSpec((None, 1, tn), lambda i, j: (layer, 0, col(i, j))),
        ],
        out_specs=pl.BlockSpec((tm, tn), lambda i, j: (tile(i), col(i, j))),
        out_shape=jax.ShapeDtypeStruct((m, n), BF16),
        scratch_shapes=[pltpu.VMEM((tm, d), BF16), pltpu.VMEM((tm, d), BF16)],
        compiler_params=_params("arbitrary", "arbitrary"),
        name="in_proj",
    )(xf, g, modr, modr, w, b)


def _spatial_gating(u_ref, v_ref, gv_ref, bv_ref, ws_ref, bst_ref, o_ref):
    groups = ws_ref.shape[0]
    n_chunks = u_ref.shape[0] // CHUNK
    rowi = lax.broadcasted_iota(jnp.int32, (CHUNK, CHUNK), 0)
    coli = lax.broadcasted_iota(jnp.int32, (CHUNK, CHUNK), 1)
    causal = rowi >= coli
    gv = gv_ref[...]
    bv = bv_ref[...]
    ws = [jnp.where(causal, ws_ref[g], 0.0).astype(BF16) for g in range(groups)]
    for c in range(n_chunks):
        rs = slice(c * CHUNK, (c + 1) * CHUNK)
        v = jax.nn.gelu(v_ref[rs, :].astype(F32))
        mu = jnp.mean(v, axis=-1, keepdims=True)
        vc = v - mu
        var = jnp.mean(vc * vc, axis=-1, keepdims=True)
        vn = (vc * lax.rsqrt(var + EPS) * gv + bv).astype(BF16)
        for g in range(groups):
            cs = slice(g * HEAD_DIM, (g + 1) * HEAD_DIM)
            fv = jnp.dot(ws[g], vn[:, cs], preferred_element_type=F32) + bst_ref[:, g:g + 1]
            u = jax.nn.gelu(u_ref[rs, cs].astype(F32))
            o_ref[rs, cs] = (u * fv).astype(o_ref.dtype)


def _attn_body(slopes_ref, *refs):
    ng = len(DILATIONS)
    q_refs = refs[0:ng]
    kc_refs = refs[ng:2 * ng]
    kp_refs = refs[2 * ng:3 * ng]
    vc_refs = refs[3 * ng:4 * ng]
    vp_refs = refs[4 * ng:5 * ng]
    o_ref = refs[5 * ng]
    qbuf, kbuf, vbuf, qbuf2, kbuf2, vbuf2 = refs[5 * ng + 1:5 * ng + 7]
    u_scr = refs[5 * ng + 7:5 * ng + 7 + ng]
    m_scr = refs[5 * ng + 7 + ng:5 * ng + 7 + 2 * ng]
    l_scr = refs[5 * ng + 7 + 2 * ng:5 * ng + 7 + 3 * ng]

    tile = pl.program_id(1)
    hg = pl.program_id(2)
    t = ATTN_TILE
    q_scale = HEAD_DIM ** -0.5 * LOG2E

    qi = lax.broadcasted_iota(jnp.int32, (Q_BLOCK, 2 * Q_BLOCK), 0) + Q_BLOCK
    ki = lax.broadcasted_iota(jnp.int32, (Q_BLOCK, 2 * Q_BLOCK), 1)
    dist = qi - ki
    in_band = (dist >= 0) & (dist <= SPAN)
    before_start = ki < Q_BLOCK

    def attend(qb, kb, vb, bias):
        s = lax.dot_general(qb, kb, (((1,), (1,)), ((), ())),
                            preferred_element_type=F32) + bias
        mx = jnp.max(s, axis=-1, keepdims=True)
        p = jnp.exp2(s - mx)
        den = jnp.sum(p, axis=-1, keepdims=True)
        u = jnp.dot(p.astype(BF16), vb, preferred_element_type=F32)
        return u, mx, den

    for g, d in enumerate(DILATIONS):
        nblk = t // (Q_BLOCK * d)
        back = Q_BLOCK * d
        q_ref, kc_ref, kp_ref, vc_ref, vp_ref = (q_refs[g], kc_refs[g], kp_refs[g],
                                                 vc_refs[g], vp_refs[g])
        slope = slopes_ref[g * HEADS_PER_GROUP + hg] * LOG2E
        bias = jnp.where(in_band, -slope * (d * dist).astype(F32), NEG_INF)
        bias_first = jnp.where(jnp.logical_and(tile == 0, before_start), NEG_INF, bias)

        if d > 1:
            rows_kv = back + t
            qbuf[...] = q_ref[...].astype(F32) * q_scale
            kbuf[0:back, :] = kp_ref[t - back:t, :].astype(F32)
            kbuf[back:rows_kv, :] = kc_ref[...].astype(F32)
            vbuf[0:back, :] = vp_ref[t - back:t, :].astype(F32)
            vbuf[back:rows_kv, :] = vc_ref[...].astype(F32)
        if d > FAST_STRIDE:
            for j in range(FAST_STRIDE):
                nq, nkv = t // FAST_STRIDE, rows_kv // FAST_STRIDE
                qbuf2[j * nq:(j + 1) * nq, :] = qbuf[pl.ds(j, nq, stride=FAST_STRIDE), :]
                kbuf2[j * nkv:(j + 1) * nkv, :] = kbuf[pl.ds(j, nkv, stride=FAST_STRIDE), :]
                vbuf2[j * nkv:(j + 1) * nkv, :] = vbuf[pl.ds(j, nkv, stride=FAST_STRIDE), :]

        for r in range(d):
            for n in range(nblk):
                if d == 1:
                    lo = n * Q_BLOCK
                    out_rows = pl.ds(lo, Q_BLOCK)
                    qb = (q_ref[out_rows, :].astype(F32) * q_scale).astype(BF16)
                    if n == 0:
                        kb = jnp.concatenate([kp_ref[t - Q_BLOCK:t, :], kc_ref[0:Q_BLOCK, :]], 0)
                        vb = jnp.concatenate([vp_ref[t - Q_BLOCK:t, :], vc_ref[0:Q_BLOCK, :]], 0)
                    else:
                        kb = kc_ref[lo - Q_BLOCK:lo + Q_BLOCK, :]
                        vb = vc_ref[lo - Q_BLOCK:lo + Q_BLOCK, :]
                else:
                    out_rows = pl.ds(r + d * Q_BLOCK * n, Q_BLOCK, stride=d)
                    if d > FAST_STRIDE:
                        j, i, d2 = r % FAST_STRIDE, r // FAST_STRIDE, d // FAST_STRIDE
                        q_lo = j * (t // FAST_STRIDE) + i + d2 * Q_BLOCK * n
                        kv_lo = j * (rows_kv // FAST_STRIDE) + i + d2 * Q_BLOCK * n
                        src = (qbuf2, kbuf2, vbuf2)
                    else:
                        d2 = d
                        q_lo = kv_lo = r + d * Q_BLOCK * n
                        src = (qbuf, kbuf, vbuf)
                    qb = src[0][pl.ds(q_lo, Q_BLOCK, stride=d2), :].astype(BF16)
                    kb = src[1][pl.ds(kv_lo, 2 * Q_BLOCK, stride=d2), :].astype(BF16)
                    vb = src[2][pl.ds(kv_lo, 2 * Q_BLOCK, stride=d2), :].astype(BF16)
                u, mx, den = attend(qb, kb, vb, bias_first if n == 0 else bias)
                u_scr[g][out_rows, :] = u
                m_scr[g][out_rows, :] = jnp.broadcast_to(mx, (Q_BLOCK, HEAD_DIM))
                l_scr[g][out_rows, :] = jnp.broadcast_to(den, (Q_BLOCK, HEAD_DIM))

    def combine(i, carry):
        rows = pl.ds(pl.multiple_of(i * Q_BLOCK, Q_BLOCK), Q_BLOCK)
        ms = [m_scr[g][rows, :] for g in range(ng)]
        top = functools.reduce(jnp.maximum, ms)
        num = jnp.zeros((Q_BLOCK, HEAD_DIM), F32)
        den = jnp.zeros((Q_BLOCK, HEAD_DIM), F32)
        for g in range(ng):
            w = jnp.exp2(ms[g] - top)
            num = num + w * u_scr[g][rows, :]
            den = den + w * l_scr[g][rows, :]
        o_ref[rows, :] = (num / den).astype(o_ref.dtype)
        return carry

    lax.fori_loop(0, t // Q_BLOCK, combine, 0, unroll=2)


def _attention(proj, slopes, bsz, col0, d_b):
    m = proj.shape[0]
    t = ATTN_TILE
    tiles = (m // bsz) // t
    ng = len(DILATIONS)
    hb = col0 // HEAD_DIM
    heads = d_b // HEAD_DIM

    def cur(base, g):
        return pl.BlockSpec(
            (t, HEAD_DIM),
            lambda b, i, h, s: (b * tiles + i, base + g * HEADS_PER_GROUP + h))

    def prev(base, g):
        return pl.BlockSpec(
            (t, HEAD_DIM),
            lambda b, i, h, s: (b * tiles + jnp.maximum(i - 1, 0),
                                base + g * HEADS_PER_GROUP + h))

    in_specs = ([cur(hb, g) for g in range(ng)]
                + [cur(hb + heads, g) for g in range(ng)]
                + [prev(hb + heads, g) for g in range(ng)]
                + [cur(hb + 2 * heads, g) for g in range(ng)]
                + [prev(hb + 2 * heads, g) for g in range(ng)])
    kv_rows = t + Q_BLOCK * max(DILATIONS)
    staging = [pltpu.VMEM((t, HEAD_DIM), F32),
               pltpu.VMEM((kv_rows, HEAD_DIM), F32),
               pltpu.VMEM((kv_rows, HEAD_DIM), F32)]
    scratch = staging + staging + [pltpu.VMEM((t, HEAD_DIM), F32) for _ in range(3 * ng)]
    grid_spec = pltpu.PrefetchScalarGridSpec(
        num_scalar_prefetch=1,
        grid=(bsz, tiles, HEADS_PER_GROUP),
        in_specs=in_specs,
        out_specs=pl.BlockSpec((t, HEAD_DIM), lambda b, i, h, s: (b * tiles + i, h)),
        scratch_shapes=scratch,
    )
    return pl.pallas_call(
        _attn_body,
        grid_spec=grid_spec,
        out_shape=jax.ShapeDtypeStruct((m, HEADS_PER_GROUP * HEAD_DIM), BF16),
        compiler_params=_params("arbitrary", "arbitrary", "arbitrary"),
        name="attention",
    )(slopes, *([proj] * (5 * ng)))


def _outproj_body(u_ref, v_ref, gv_ref, bv_ref, ws_ref, bst_ref, yb_ref, *refs, gate_pieces):
    ga_refs = refs[:gate_pieces]
    gb_refs = refs[gate_pieces:2 * gate_pieces]
    x_ref, gt_ref, woa_ref, wob_ref, wout_ref, o_ref, ya0_ref, ya1_ref = refs[2 * gate_pieces:]
    s = pl.program_id(0)

    def gating(dst_ref):
        _spatial_gating(u_ref, v_ref, gv_ref, bv_ref, ws_ref, bst_ref, dst_ref)

    def project(ya_ref):
        a = jnp.dot(ya_ref[...], woa_ref[...], preferred_element_type=F32)
        b = jnp.dot(yb_ref[...], wob_ref[...], preferred_element_type=F32)
        gw = ga_refs[0].shape[1]
        merged = jnp.concatenate(
            [(jax.nn.sigmoid(ga_refs[k][...].astype(F32)) * a[:, k * gw:(k + 1) * gw]
              + jax.nn.sigmoid(gb_refs[k][...].astype(F32)) * b[:, k * gw:(k + 1) * gw]
              ).astype(BF16) for k in range(gate_pieces)], axis=1)
        o = jnp.dot(merged, wout_ref[...], preferred_element_type=F32)
        o_ref[...] = x_ref[...] + gt_ref[...] * o

    @pl.when(s == 0)
    def _():
        gating(ya0_ref)

    @pl.when(jnp.logical_and(s > 0, s % 2 == 1))
    def _():
        project(ya0_ref)
        gating(ya1_ref)

    @pl.when(jnp.logical_and(s > 0, s % 2 == 0))
    def _():
        project(ya1_ref)
        gating(ya0_ref)


def _out_proj(yb, proj, xf, modr, g_v, b_v, w_s, b_st, w_oa, w_ob, w_out, layer, bsz, gate_col0,
              tm=512):
    m, d = xf.shape
    d_a = w_oa.shape[1]
    d_bo = yb.shape[1]
    groups = w_s.shape[1]
    tpb = (m // bsz) // tm
    gw = math.gcd(gate_col0, d)
    pieces = d // gw
    lay3 = lambda i: (layer, 0, 0)

    n_tiles = m // tm
    ahead = lambda s: jnp.minimum(s, n_tiles - 1)
    tile = lambda s: jnp.maximum(s - 1, 0)

    def gate(col0):
        return [pl.BlockSpec((tm, gw), lambda s, c=col0 // gw + k: (tile(s), c))
                for k in range(pieces)]

    return pl.pallas_call(
        functools.partial(_outproj_body, gate_pieces=pieces),
        grid=(n_tiles + 1,),
        in_specs=[
            pl.BlockSpec((tm, d_a), lambda s: (ahead(s), 0)),
            pl.BlockSpec((tm, d_a), lambda s: (ahead(s), 1)),
            pl.BlockSpec((None, 1, d_a), lay3),
            pl.BlockSpec((None, 1, d_a), lay3),
            pl.BlockSpec((None, groups, CHUNK, CHUNK), lambda s: (layer, 0, 0, 0)),
            pl.BlockSpec((None, CHUNK, groups), lay3),
            pl.BlockSpec((tm, d_bo), lambda s: (tile(s), 0)),
            *gate(gate_col0),
            *gate(gate_col0 + d),
            pl.BlockSpec((tm, d), lambda s: (tile(s), 0)),
            pl.BlockSpec((None, 1, d),
                         lambda s: ((layer * bsz + tile(s) // tpb) * 6 + 2, 0, 0)),
            pl.BlockSpec((None, d_a, d), lay3, pipeline_mode=pl.Buffered(1)),
            pl.BlockSpec((None, d_bo, d), lay3, pipeline_mode=pl.Buffered(1)),
            pl.BlockSpec((None, d, d), lay3, pipeline_mode=pl.Buffered(1)),
        ],
        out_specs=pl.BlockSpec((tm, d), lambda s: (tile(s), 0)),
        out_shape=jax.ShapeDtypeStruct((m, d), F32),
        scratch_shapes=[pltpu.VMEM((tm, d_a), BF16), pltpu.VMEM((tm, d_a), BF16)],
        compiler_params=_params("arbitrary", vmem=60 * MIB),
        name="out_proj",
    )(proj, proj, g_v, b_v, w_s, b_st, yb, *([proj] * (2 * pieces)), xf, modr, w_oa, w_ob, w_out)


def _mlp_body(x_ref, g_ref, sh_ref, sc_ref, gt_ref, w1_ref, b1_ref, w2_ref, b2_ref, gf_ref,
              o_ref, h_ref, inv_ref, *, final_norm, rows=16):
    f = pl.program_id(1)
    n_row_steps = x_ref.shape[0] // rows

    def partial_sum():
        hid = jnp.dot(h_ref[...], w1_ref[...], preferred_element_type=F32) + b1_ref[...]
        hid = jnp.square(jnp.maximum(hid, 0.0)).astype(BF16)
        return jnp.dot(hid, w2_ref[...], preferred_element_type=F32)

    @pl.when(f == 0)
    def _():
        _norm_modulate(h_ref, x_ref, g_ref, sh_ref, sc_ref)
        o_ref[...] = partial_sum()

    @pl.when(f > 0)
    def _():
        o_ref[...] += partial_sum()

    @pl.when(f == pl.num_programs(1) - 1)
    def _():
        gt = gt_ref[...]
        b2 = b2_ref[...]

        def residual(i, carry):
            r = pl.ds(pl.multiple_of(i * rows, rows), rows)
            y = x_ref[r, :] + gt * (o_ref[r, :] + b2)
            o_ref[r, :] = y
            if final_norm:
                ms = jnp.mean(y * y, axis=-1, keepdims=True)
                inv_ref[r, :] = jnp.broadcast_to(lax.rsqrt(ms + EPS), (rows, inv_ref.shape[1]))
            return carry

        lax.fori_loop(0, n_row_steps, residual, 0, unroll=8)

        if final_norm:
            gf = gf_ref[...]
            reps = o_ref.shape[1] // inv_ref.shape[1]

            def normalise(i, carry):
                r = pl.ds(pl.multiple_of(i * rows, rows), rows)
                inv = jnp.concatenate([inv_ref[r, :]] * reps, axis=1)
                o_ref[r, :] = o_ref[r, :] * inv * gf
                return carry

            lax.fori_loop(0, n_row_steps, normalise, 0, unroll=8)


def _mlp(xf, g, modr, w1, b1, w2, b2, g_final, layer, bsz, final_norm, tm=1024, tf=1024):
    m, d = xf.shape
    tpb = (m // bsz) // tm
    lay3 = lambda i, f: (layer, 0, 0)
    return pl.pallas_call(
        functools.partial(_mlp_body, final_norm=final_norm),
        grid=(m // tm, w1.shape[2] // tf),
        in_specs=[
            pl.BlockSpec((tm, d), lambda i, f: (i, 0)),
            pl.BlockSpec((None, 1, d), lay3),
            _mod_spec(d, layer, bsz, 3, tpb),
            _mod_spec(d, layer, bsz, 4, tpb),
            _mod_spec(d, layer, bsz, 5, tpb),
            pl.BlockSpec((None, d, tf), lambda i, f: (layer, 0, f)),
            pl.BlockSpec((None, 1, tf), lambda i, f: (layer, 0, f)),
            pl.BlockSpec((None, tf, d), lambda i, f: (layer, f, 0)),
            pl.BlockSpec((None, 1, d), lay3),
            pl.BlockSpec((1, d), lambda i, f: (0, 0)),
        ],
        out_specs=pl.BlockSpec((tm, d), lambda i, f: (i, 0)),
        out_shape=jax.ShapeDtypeStruct((m, d), F32),
        scratch_shapes=[pltpu.VMEM((tm, d), BF16), pltpu.VMEM((tm, LANES), F32)],
        compiler_params=_params("arbitrary", "arbitrary", vmem=60 * MIB),
        name="mlp",
    )(xf, g, modr, modr, modr, w1, b1, w2, b2, g_final)


def kernel(x, c, w_ada, b_ada, g_mix, w_in, b_in, g_v, b_v, w_s, b_s, w_oa, w_ob, w_out,
           g_mlp, w1, b1, w2, b2, g_final):
    bsz, seq, d = x.shape
    depth = w_ada.shape[0]
    m = bsz * seq
    d_a = w_oa.shape[1]
    d_bo = w_ob.shape[1]
    d_b = d_bo * len(DILATIONS)
    assert seq % ATTN_TILE == 0 and d_bo == HEADS_PER_GROUP * HEAD_DIM

    qkv_col0 = 2 * d_a
    gate_col0 = qkv_col0 + 3 * d_b

    n_heads = d_b // HEAD_DIM
    slopes = 2.0 ** (-ALIBI_MAX_EXP * jnp.arange(1, n_heads + 1, dtype=F32) / n_heads)

    def row(a):
        return a.reshape(depth, 1, a.shape[1])

    w_in_b, w_oa_b, w_ob_b, w_out_b, w1_b, w2_b = (
        a.astype(BF16) for a in (w_in, w_oa, w_ob, w_out, w1, w2))
    b_st = jnp.swapaxes(b_s, 1, 2)

    mod = _ada(c, w_ada, b_ada)
    modr = mod.reshape(depth * bsz * 6, 1, d)
    xf = x.reshape(m, d)
    for l in range(depth):
        proj = _in_proj(xf, row(g_mix), modr, w_in_b, row(b_in), l, bsz)
        yb = _attention(proj, slopes, bsz, qkv_col0, d_b)
        x1 = _out_proj(yb, proj, xf, modr, row(g_v), row(b_v), w_s, b_st, w_oa_b, w_ob_b, w_out_b,
                       l, bsz, gate_col0)
        xf = _mlp(x1, row(g_mlp), modr, w1_b, row(b1), w2_b, row(b2), g_final.reshape(1, d),
                  l, bsz, final_norm=(l == depth - 1))
    return xf.reshape(bsz, seq, d)
```

```python
import functools
import math

import jax
import jax.numpy as jnp
from jax import lax
from jax.experimental import pallas as pl
from jax.experimental.pallas import tpu as pltpu

F32 = jnp.float32
BF16 = jnp.bfloat16

HEAD_DIM = 128
CHUNK = 128
Q_BLOCK = 128
SPAN = 128
DILATIONS = (1, 4, 16)
HEADS_PER_GROUP = 4
ALIBI_MAX_EXP = 8.0
EPS = 1e-6
ATTN_TILE = Q_BLOCK * max(DILATIONS)
NEG_INF = float("-inf")
LOG2E = 1.4426950408889634
FAST_STRIDE = 4
LANES = 128
MIB = 1024 * 1024
VMEM_LIMIT = 56 * MIB


def _params(*sem, vmem=VMEM_LIMIT):
    return pltpu.CompilerParams(dimension_semantics=sem, vmem_limit_bytes=vmem)


def _ada_body(c_ref, w_ref, b_ref, o_ref):
    c = c_ref[...]
    act = (c * jax.nn.sigmoid(c)).astype(BF16)
    o_ref[...] = jnp.dot(act, w_ref[...].astype(BF16), preferred_element_type=F32) + b_ref[...]


def _ada(c, w_ada, b_ada):
    depth, d, n = w_ada.shape
    bsz = c.shape[0]
    tn = 1024
    return pl.pallas_call(
        _ada_body,
        grid=(depth, n // tn),
        in_specs=[
            pl.BlockSpec((bsz, d), lambda l, j: (0, 0)),
            pl.BlockSpec((None, d, tn), lambda l, j: (l, 0, j)),
            pl.BlockSpec((None, 1, tn), lambda l, j: (l, 0, j)),
        ],
        out_specs=pl.BlockSpec((None, bsz, tn), lambda l, j: (l, 0, j)),
        out_shape=jax.ShapeDtypeStruct((depth, bsz, n), F32),
        compiler_params=_params("arbitrary", "arbitrary"),
        name="ada",
    )(c, w_ada, b_ada.reshape(depth, 1, n))


def _norm_rows(h_ref, x_ref, gs, sh, r):
    xv = x_ref[r, :]
    ms = jnp.mean(xv * xv, axis=-1, keepdims=True)
    h_ref[r, :] = (xv * lax.rsqrt(ms + EPS) * gs + sh).astype(h_ref.dtype)


def _norm_modulate(h_ref, x_ref, g_ref, sh_ref, sc_ref, rows=16, unroll=16):
    gs = g_ref[...] * (1.0 + sc_ref[...])
    sh = sh_ref[...]

    def body(i, carry):
        _norm_rows(h_ref, x_ref, gs, sh, pl.ds(pl.multiple_of(i * rows, rows), rows))
        return carry

    lax.fori_loop(0, x_ref.shape[0] // rows, body, 0, unroll=unroll)


def _mod_spec(d, layer, bsz, comp, tiles_per_batch):
    return pl.BlockSpec(
        (None, 1, d),
        lambda i, *_: ((layer * bsz + i // tiles_per_batch) * 6 + comp, 0, 0))


def _inproj_body(x_ref, g_ref, sh_ref, sc_ref, w_ref, b_ref, o_ref, h0_ref, h1_ref, *,
                 rows=16):
    i = pl.program_id(0)
    j = pl.program_id(1)
    tm = x_ref.shape[0]
    per_step = -(-tm // (pl.num_programs(1) * rows)) * rows

    def normalise(h_ref):
        gs = g_ref[...] * (1.0 + sc_ref[...])
        sh = sh_ref[...]
        start = pl.multiple_of(jnp.minimum(j * per_step, tm - per_step), rows)
        for k in range(per_step // rows):
            _norm_rows(h_ref, x_ref, gs, sh, pl.ds(start + k * rows, rows))

    def project(h_ref):
        acc = jnp.dot(h_ref[...], w_ref[...], preferred_element_type=F32)
        o_ref[...] = (acc + b_ref[...]).astype(o_ref.dtype)

    @pl.when(i == 0)
    def _():
        normalise(h0_ref)

    @pl.when(jnp.logical_and(i > 0, i % 2 == 1))
    def _():
        project(h0_ref)
        normalise(h1_ref)

    @pl.when(jnp.logical_and(i > 0, i % 2 == 0))
    def _():
        project(h1_ref)
        normalise(h0_ref)


def _in_proj(xf, g, modr, w, b, layer, bsz, tm=1024, tn=1792):
    m, d = xf.shape
    n = w.shape[2]
    n_tiles = m // tm
    tpb = (m // bsz) // tm
    ahead = lambda i: jnp.minimum(i, n_tiles - 1)
    tile = lambda i: jnp.maximum(i - 1, 0)
    col = lambda i, j: jnp.where(i == 0, 0, j)

    def mod(comp):
        return pl.BlockSpec(
            (None, 1, d), lambda i, j: ((layer * bsz + ahead(i) // tpb) * 6 + comp, 0, 0))

    return pl.pallas_call(
        _inproj_body,
        grid=(n_tiles + 1, n // tn),
        in_specs=[
            pl.BlockSpec((tm, d), lambda i, j: (ahead(i), 0)),
            pl.BlockSpec((None, 1, d), lambda i, j: (layer, 0, 0)),
            mod(0),
            mod(1),
            pl.BlockSpec((None, d, tn), lambda i, j: (layer, 0, col(i, j))),
            pl.BlockSpec((None, 1, tn), lambda i, j: (layer, 0, col(i, j))),
        ],
        out_specs=pl.BlockSpec((tm, tn), lambda i, j: (tile(i), col(i, j))),
        out_shape=jax.ShapeDtypeStruct((m, n), BF16),
        scratch_shapes=[pltpu.VMEM((tm, d), BF16), pltpu.VMEM((tm, d), BF16)],
        compiler_params=_params("arbitrary", "arbitrary"),
        name="in_proj",
    )(xf, g, modr, modr, w, b)


def _spatial_gating(u_ref, v_ref, gv_ref, bv_ref, ws_ref, bst_ref, o_ref):
    groups = ws_ref.shape[0]
    n_chunks = u_ref.shape[0] // CHUNK
    rowi = lax.broadcasted_iota(jnp.int32, (CHUNK, CHUNK), 0)
    coli = lax.broadcasted_iota(jnp.int32, (CHUNK, CHUNK), 1)
    causal = rowi >= coli
    gv = gv_ref[...]
    bv = bv_ref[...]
    ws = [jnp.where(causal, ws_ref[g], 0.0).astype(BF16) for g in range(groups)]
    for c in range(n_chunks):
        rs = slice(c * CHUNK, (c + 1) * CHUNK)
        v = jax.nn.gelu(v_ref[rs, :].astype(F32))
        mu = jnp.mean(v, axis=-1, keepdims=True)
        vc = v - mu
        var = jnp.mean(vc * vc, axis=-1, keepdims=True)
        vn = (vc * lax.rsqrt(var + EPS) * gv + bv).astype(BF16)
        for g in range(groups):
            cs = slice(g * HEAD_DIM, (g + 1) * HEAD_DIM)
            fv = jnp.dot(ws[g], vn[:, cs], preferred_element_type=F32) + bst_ref[:, g:g + 1]
            u = jax.nn.gelu(u_ref[rs, cs].astype(F32))
            o_ref[rs, cs] = (u * fv).astype(o_ref.dtype)


def _attn_body(slopes_ref, *refs):
    ng = len(DILATIONS)
    q_refs = refs[0:ng]
    kc_refs = refs[ng:2 * ng]
    kp_refs = refs[2 * ng:3 * ng]
    vc_refs = refs[3 * ng:4 * ng]
    vp_refs = refs[4 * ng:5 * ng]
    o_ref = refs[5 * ng]
    qbuf, kbuf, vbuf, qbuf2, kbuf2, vbuf2, ubuf2, mbuf2, lbuf2 = refs[5 * ng + 1:5 * ng + 10]
    u_scr = refs[5 * ng + 10:5 * ng + 10 + ng]
    m_scr = refs[5 * ng + 10 + ng:5 * ng + 10 + 2 * ng]
    l_scr = refs[5 * ng + 10 + 2 * ng:5 * ng + 10 + 3 * ng]

    tile = pl.program_id(1)
    hg = pl.program_id(2)
    t = ATTN_TILE
    q_scale = HEAD_DIM ** -0.5 * LOG2E

    qi = lax.broadcasted_iota(jnp.int32, (Q_BLOCK, 2 * Q_BLOCK), 0) + Q_BLOCK
    ki = lax.broadcasted_iota(jnp.int32, (Q_BLOCK, 2 * Q_BLOCK), 1)
    dist = qi - ki
    in_band = (dist >= 0) & (dist <= SPAN)
    before_start = ki < Q_BLOCK

    def attend(qb, kb, vb, bias):
        s = lax.dot_general(qb, kb, (((1,), (1,)), ((), ())),
                            preferred_element_type=F32) + bias
        mx = jnp.max(s, axis=-1, keepdims=True)
        p = jnp.exp2(s - mx)
        den = jnp.sum(p, axis=-1, keepdims=True)
        u = jnp.dot(p.astype(BF16), vb, preferred_element_type=F32)
        return u, mx, den

    for g, d in enumerate(DILATIONS):
        nblk = t // (Q_BLOCK * d)
        back = Q_BLOCK * d
        q_ref, kc_ref, kp_ref, vc_ref, vp_ref = (q_refs[g], kc_refs[g], kp_refs[g],
                                                 vc_refs[g], vp_refs[g])
        slope = slopes_ref[g * HEADS_PER_GROUP + hg] * LOG2E
        bias = jnp.where(in_band, -slope * (d * dist).astype(F32), NEG_INF)
        bias_first = jnp.where(jnp.logical_and(tile == 0, before_start), NEG_INF, bias)

        if d > 1:
            rows_kv = back + t
            qbuf[...] = q_ref[...].astype(F32) * q_scale
            kbuf[0:back, :] = kp_ref[t - back:t, :].astype(F32)
            kbuf[back:rows_kv, :] = kc_ref[...].astype(F32)
            vbuf[0:back, :] = vp_ref[t - back:t, :].astype(F32)
            vbuf[back:rows_kv, :] = vc_ref[...].astype(F32)
        if d > FAST_STRIDE:
            for j in range(FAST_STRIDE):
                nq, nkv = t // FAST_STRIDE, rows_kv // FAST_STRIDE
                qbuf2[j * nq:(j + 1) * nq, :] = qbuf[pl.ds(j, nq, stride=FAST_STRIDE), :]
                kbuf2[j * nkv:(j + 1) * nkv, :] = kbuf[pl.ds(j, nkv, stride=FAST_STRIDE), :]
                vbuf2[j * nkv:(j + 1) * nkv, :] = vbuf[pl.ds(j, nkv, stride=FAST_STRIDE), :]

        for r in range(d):
            for n in range(nblk):
                if d == 1:
                    lo = n * Q_BLOCK
                    out_rows = pl.ds(lo, Q_BLOCK)
                    qb = (q_ref[out_rows, :].astype(F32) * q_scale).astype(BF16)
                    if n == 0:
                        kb = jnp.concatenate([kp_ref[t - Q_BLOCK:t, :], kc_ref[0:Q_BLOCK, :]], 0)
                        vb = jnp.concatenate([vp_ref[t - Q_BLOCK:t, :], vc_ref[0:Q_BLOCK, :]], 0)
                    else:
                        kb = kc_ref[lo - Q_BLOCK:lo + Q_BLOCK, :]
                        vb = vc_ref[lo - Q_BLOCK:lo + Q_BLOCK, :]
                    dst = (u_scr[g], m_scr[g], l_scr[g])
                else:
                    if d > FAST_STRIDE:
                        j, i, d2 = r % FAST_STRIDE, r // FAST_STRIDE, d // FAST_STRIDE
                        q_lo = j * (t // FAST_STRIDE) + i + d2 * Q_BLOCK * n
                        kv_lo = j * (rows_kv // FAST_STRIDE) + i + d2 * Q_BLOCK * n
                        src = (qbuf2, kbuf2, vbuf2)
                        dst = (ubuf2, mbuf2, lbuf2)
                    else:
                        d2 = d
                        q_lo = kv_lo = r + d * Q_BLOCK * n
                        src = (qbuf, kbuf, vbuf)
                        dst = (u_scr[g], m_scr[g], l_scr[g])
                    out_rows = pl.ds(q_lo, Q_BLOCK, stride=d2)
                    qb = src[0][out_rows, :].astype(BF16)
                    kb = src[1][pl.ds(kv_lo, 2 * Q_BLOCK, stride=d2), :].astype(BF16)
                    vb = src[2][pl.ds(kv_lo, 2 * Q_BLOCK, stride=d2), :].astype(BF16)
                u, mx, den = attend(qb, kb, vb, bias_first if n == 0 else bias)
                dst[0][out_rows, :] = u
                dst[1][out_rows, :] = jnp.broadcast_to(mx, (Q_BLOCK, HEAD_DIM))
                dst[2][out_rows, :] = jnp.broadcast_to(den, (Q_BLOCK, HEAD_DIM))

        if d > FAST_STRIDE:
            nq = t // FAST_STRIDE
            for j in range(FAST_STRIDE):
                for two_level, natural in ((ubuf2, u_scr[g]), (mbuf2, m_scr[g]),
                                           (lbuf2, l_scr[g])):
                    natural[pl.ds(j, nq, stride=FAST_STRIDE), :] = two_level[j * nq:(j + 1) * nq, :]

    def combine(i, carry):
        rows = pl.ds(pl.multiple_of(i * Q_BLOCK, Q_BLOCK), Q_BLOCK)
        ms = [m_scr[g][rows, :] for g in range(ng)]
        top = functools.reduce(jnp.maximum, ms)
        num = jnp.zeros((Q_BLOCK, HEAD_DIM), F32)
        den = jnp.zeros((Q_BLOCK, HEAD_DIM), F32)
        for g in range(ng):
            w = jnp.exp2(ms[g] - top)
            num = num + w * u_scr[g][rows, :]
            den = den + w * l_scr[g][rows, :]
        o_ref[rows, :] = (num / den).astype(o_ref.dtype)
        return carry

    lax.fori_loop(0, t // Q_BLOCK, combine, 0, unroll=2)


def _attention(proj, slopes, bsz, col0, d_b):
    m = proj.shape[0]
    t = ATTN_TILE
    tiles = (m // bsz) // t
    ng = len(DILATIONS)
    hb = col0 // HEAD_DIM
    heads = d_b // HEAD_DIM

    def cur(base, g):
        return pl.BlockSpec(
            (t, HEAD_DIM),
            lambda b, i, h, s: (b * tiles + i, base + g * HEADS_PER_GROUP + h))

    def prev(base, g):
        return pl.BlockSpec(
            (t, HEAD_DIM),
            lambda b, i, h, s: (b * tiles + jnp.maximum(i - 1, 0),
                                base + g * HEADS_PER_GROUP + h))

    in_specs = ([cur(hb, g) for g in range(ng)]
                + [cur(hb + heads, g) for g in range(ng)]
                + [prev(hb + heads, g) for g in range(ng)]
                + [cur(hb + 2 * heads, g) for g in range(ng)]
                + [prev(hb + 2 * heads, g) for g in range(ng)])
    kv_rows = t + Q_BLOCK * max(DILATIONS)
    staging = [pltpu.VMEM((t, HEAD_DIM), F32),
               pltpu.VMEM((kv_rows, HEAD_DIM), F32),
               pltpu.VMEM((kv_rows, HEAD_DIM), F32)]
    scratch = staging + staging + [pltpu.VMEM((t, HEAD_DIM), F32) for _ in range(3 + 3 * ng)]
    grid_spec = pltpu.PrefetchScalarGridSpec(
        num_scalar_prefetch=1,
        grid=(bsz, tiles, HEADS_PER_GROUP),
        in_specs=in_specs,
        out_specs=pl.BlockSpec((t, HEAD_DIM), lambda b, i, h, s: (b * tiles + i, h)),
        scratch_shapes=scratch,
    )
    return pl.pallas_call(
        _attn_body,
        grid_spec=grid_spec,
        out_shape=jax.ShapeDtypeStruct((m, HEADS_PER_GROUP * HEAD_DIM), BF16),
        compiler_params=_params("arbitrary", "arbitrary", "arbitrary"),
        name="attention",
    )(slopes, *([proj] * (5 * ng)))


def _outproj_body(u_ref, v_ref, gv_ref, bv_ref, ws_ref, bst_ref, yb_ref, *refs, gate_pieces):
    ga_refs = refs[:gate_pieces]
    gb_refs = refs[gate_pieces:2 * gate_pieces]
    x_ref, gt_ref, woa_ref, wob_ref, wout_ref, o_ref, ya0_ref, ya1_ref = refs[2 * gate_pieces:]
    s = pl.program_id(0)

    def gating(dst_ref):
        _spatial_gating(u_ref, v_ref, gv_ref, bv_ref, ws_ref, bst_ref, dst_ref)

    def project(ya_ref):
        a = jnp.dot(ya_ref[...], woa_ref[...], preferred_element_type=F32)
        b = jnp.dot(yb_ref[...], wob_ref[...], preferred_element_type=F32)
        gw = ga_refs[0].shape[1]
        merged = jnp.concatenate(
            [(jax.nn.sigmoid(ga_refs[k][...].astype(F32)) * a[:, k * gw:(k + 1) * gw]
              + jax.nn.sigmoid(gb_refs[k][...].astype(F32)) * b[:, k * gw:(k + 1) * gw]
              ).astype(BF16) for k in range(gate_pieces)], axis=1)
        o = jnp.dot(merged, wout_ref[...], preferred_element_type=F32)
        o_ref[...] = x_ref[...] + gt_ref[...] * o

    @pl.when(s == 0)
    def _():
        gating(ya0_ref)

    @pl.when(jnp.logical_and(s > 0, s % 2 == 1))
    def _():
        project(ya0_ref)
        gating(ya1_ref)

    @pl.when(jnp.logical_and(s > 0, s % 2 == 0))
    def _():
        project(ya1_ref)
        gating(ya0_ref)


def _out_proj(yb, proj, xf, modr, g_v, b_v, w_s, b_st, w_oa, w_ob, w_out, layer, bsz, gate_col0,
              tm=512):
    m, d = xf.shape
    d_a = w_oa.shape[1]
    d_bo = yb.shape[1]
    groups = w_s.shape[1]
    tpb = (m // bsz) // tm
    gw = math.gcd(gate_col0, d)
    pieces = d // gw
    lay3 = lambda i: (layer, 0, 0)

    n_tiles = m // tm
    ahead = lambda s: jnp.minimum(s, n_tiles - 1)
    tile = lambda s: jnp.maximum(s - 1, 0)

    def gate(col0):
        return [pl.BlockSpec((tm, gw), lambda s, c=col0 // gw + k: (tile(s), c))
                for k in range(pieces)]

    return pl.pallas_call(
        functools.partial(_outproj_body, gate_pieces=pieces),
        grid=(n_tiles + 1,),
        in_specs=[
            pl.BlockSpec((tm, d_a), lambda s: (ahead(s), 0)),
            pl.BlockSpec((tm, d_a), lambda s: (ahead(s), 1)),
            pl.BlockSpec((None, 1, d_a), lay3),
            pl.BlockSpec((None, 1, d_a), lay3),
            pl.BlockSpec((None, groups, CHUNK, CHUNK), lambda s: (layer, 0, 0, 0)),
            pl.BlockSpec((None, CHUNK, groups), lay3),
            pl.BlockSpec((tm, d_bo), lambda s: (tile(s), 0)),
            *gate(gate_col0),
            *gate(gate_col0 + d),
            pl.BlockSpec((tm, d), lambda s: (tile(s), 0)),
            pl.BlockSpec((None, 1, d),
                         lambda s: ((layer * bsz + tile(s) // tpb) * 6 + 2, 0, 0)),
            pl.BlockSpec((None, d_a, d), lay3, pipeline_mode=pl.Buffered(1)),
            pl.BlockSpec((None, d_bo, d), lay3, pipeline_mode=pl.Buffered(1)),
            pl.BlockSpec((None, d, d), lay3, pipeline_mode=pl.Buffered(1)),
        ],
        out_specs=pl.BlockSpec((tm, d), lambda s: (tile(s), 0)),
        out_shape=jax.ShapeDtypeStruct((m, d), F32),
        scratch_shapes=[pltpu.VMEM((tm, d_a), BF16), pltpu.VMEM((tm, d_a), BF16)],
        compiler_params=_params("arbitrary", vmem=60 * MIB),
        name="out_proj",
    )(proj, proj, g_v, b_v, w_s, b_st, yb, *([proj] * (2 * pieces)), xf, modr, w_oa, w_ob, w_out)


def _mlp_body(x_ref, g_ref, sh_ref, sc_ref, gt_ref, w1_ref, b1_ref, w2_ref, b2_ref, gf_ref,
              o_ref, h_ref, inv_ref, *, final_norm, rows=16):
    f = pl.program_id(1)
    n_row_steps = x_ref.shape[0] // rows

    def partial_sum():
        hid = jnp.dot(h_ref[...], w1_ref[...], preferred_element_type=F32) + b1_ref[...]
        hid = jnp.square(jnp.maximum(hid, 0.0)).astype(BF16)
        return jnp.dot(hid, w2_ref[...], preferred_element_type=F32)

    @pl.when(f == 0)
    def _():
        _norm_modulate(h_ref, x_ref, g_ref, sh_ref, sc_ref)
        o_ref[...] = partial_sum()

    @pl.when(f > 0)
    def _():
        o_ref[...] += partial_sum()

    @pl.when(f == pl.num_programs(1) - 1)
    def _():
        gt = gt_ref[...]
        b2 = b2_ref[...]

        def residual(i, carry):
            r = pl.ds(pl.multiple_of(i * rows, rows), rows)
            y = x_ref[r, :] + gt * (o_ref[r, :] + b2)
            o_ref[r, :] = y
            if final_norm:
                ms = jnp.mean(y * y, axis=-1, keepdims=True)
                inv_ref[r, :] = jnp.broadcast_to(lax.rsqrt(ms + EPS), (rows, inv_ref.shape[1]))
            return carry

        lax.fori_loop(0, n_row_steps, residual, 0, unroll=8)

        if final_norm:
            gf = gf_ref[...]
            reps = o_ref.shape[1] // inv_ref.shape[1]

            def normalise(i, carry):
                r = pl.ds(pl.multiple_of(i * rows, rows), rows)
                inv = jnp.concatenate([inv_ref[r, :]] * reps, axis=1)
                o_ref[r, :] = o_ref[r, :] * inv * gf
                return carry

            lax.fori_loop(0, n_row_steps, normalise, 0, unroll=8)


def _mlp(xf, g, modr, w1, b1, w2, b2, g_final, layer, bsz, final_norm, tm=1024, tf=1024):
    m, d = xf.shape
    tpb = (m // bsz) // tm
    lay3 = lambda i, f: (layer, 0, 0)
    return pl.pallas_call(
        functools.partial(_mlp_body, final_norm=final_norm),
        grid=(m // tm, w1.shape[2] // tf),
        in_specs=[
            pl.BlockSpec((tm, d), lambda i, f: (i, 0)),
            pl.BlockSpec((None, 1, d), lay3),
            _mod_spec(d, layer, bsz, 3, tpb),
            _mod_spec(d, layer, bsz, 4, tpb),
            _mod_spec(d, layer, bsz, 5, tpb),
            pl.BlockSpec((None, d, tf), lambda i, f: (layer, 0, f)),
            pl.BlockSpec((None, 1, tf), lambda i, f: (layer, 0, f)),
            pl.BlockSpec((None, tf, d), lambda i, f: (layer, f, 0)),
            pl.BlockSpec((None, 1, d), lay3),
            pl.BlockSpec((1, d), lambda i, f: (0, 0)),
        ],
        out_specs=pl.BlockSpec((tm, d), lambda i, f: (i, 0)),
        out_shape=jax.ShapeDtypeStruct((m, d), F32),
        scratch_shapes=[pltpu.VMEM((tm, d), BF16), pltpu.VMEM((tm, LANES), F32)],
        compiler_params=_params("arbitrary", "arbitrary", vmem=60 * MIB),
        name="mlp",
    )(xf, g, modr, modr, modr, w1, b1, w2, b2, g_final)


def kernel(x, c, w_ada, b_ada, g_mix, w_in, b_in, g_v, b_v, w_s, b_s, w_oa, w_ob, w_out,
           g_mlp, w1, b1, w2, b2, g_final):
    bsz, seq, d = x.shape
    depth = w_ada.shape[0]
    m = bsz * seq
    d_a = w_oa.shape[1]
    d_bo = w_ob.shape[1]
    d_b = d_bo * len(DILATIONS)
    assert seq % ATTN_TILE == 0 and d_bo == HEADS_PER_GROUP * HEAD_DIM

    qkv_col0 = 2 * d_a
    gate_col0 = qkv_col0 + 3 * d_b

    n_heads = d_b // HEAD_DIM
    slopes = 2.0 ** (-ALIBI_MAX_EXP * jnp.arange(1, n_heads + 1, dtype=F32) / n_heads)

    def row(a):
        return a.reshape(depth, 1, a.shape[1])

    w_in_b, w_oa_b, w_ob_b, w_out_b, w1_b, w2_b = (
        a.astype(BF16) for a in (w_in, w_oa, w_ob, w_out, w1, w2))
    b_st = jnp.swapaxes(b_s, 1, 2)

    mod = _ada(c, w_ada, b_ada)
    modr = mod.reshape(depth * bsz * 6, 1, d)
    xf = x.reshape(m, d)
    for l in range(depth):
        proj = _in_proj(xf, row(g_mix), modr, w_in_b, row(b_in), l, bsz)
        yb = _attention(proj, slopes, bsz, qkv_col0, d_b)
        x1 = _out_proj(yb, proj, xf, modr, row(g_v), row(b_v), w_s, b_st, w_oa_b, w_ob_b, w_out_b,
                       l, bsz, gate_col0)
        xf = _mlp(x1, row(g_mlp), modr, w1_b, row(b1), w2_b, row(b2), g_final.reshape(1, d),
                  l, bsz, final_norm=(l == depth - 1))
    return xf.reshape(bsz, seq, d)
```
